```python
import jax, jax.numpy as jnp
from jax import lax
import numpy as np

D_MODEL = 1024
BATCH = 8
SEQ = 4096
DEPTH = 1

PLE_DIM = 256
D_LRU = 512
LRU_BLOCKS = 8
LRU_BW = D_LRU // LRU_BLOCKS
CONV_W = 4
LRU_C = 8.0
D_RET = 512
RET_HEADS = 4
RET_HD = D_RET // RET_HEADS
RET_CHUNK = 128
ROPE_BASE = 10000.0
D_MIX = D_LRU + D_RET
IN_COLS = 2 * D_LRU + 4 * D_RET
SPLITS = (D_LRU, 2 * D_LRU, 2 * D_LRU + D_RET, 2 * D_LRU + 2 * D_RET, 2 * D_LRU + 3 * D_RET)
N_GROUPS = 8
EXPERTS_PER_GROUP = 8
N_EXPERTS = N_GROUPS * EXPERTS_PER_GROUP
TOP_K = 2
D_EXPERT = 512
MOE_BLOCK = 128
EPS = 1e-6

kernel_name = "hymba_rglru_retention_hmoe_ple"

F32 = jnp.float32


def rms_norm(x, g):
    xf = x.astype(F32)
    y = xf * lax.rsqrt(jnp.mean(xf * xf, axis=-1, keepdims=True) + EPS)
    return (y * g.astype(F32)).astype(x.dtype)


def causal_depthwise_conv(x, w, b):
    S = x.shape[1]
    xp = jnp.pad(x, ((0, 0), (CONV_W - 1, 0), (0, 0)))
    y = b + xp[:, 0:S, :] * w[0]
    for k in range(1, CONV_W):
        y = y + xp[:, k:k + S, :] * w[k]
    return y


def rg_lru(x, w_a, b_a, w_x, b_x, lam):
    B, S, _ = x.shape
    xf = x.astype(F32)
    xb = xf.reshape(B, S, LRU_BLOCKS, LRU_BW)
    r = jax.nn.sigmoid(jnp.einsum('bshi,hij->bshj', xb, w_a.astype(F32)) + b_a.astype(F32)).reshape(B, S, D_LRU)
    i = jax.nn.sigmoid(jnp.einsum('bshi,hij->bshj', xb, w_x.astype(F32)) + b_x.astype(F32)).reshape(B, S, D_LRU)
    log_a = -LRU_C * r * jax.nn.softplus(-lam.astype(F32))
    a = jnp.exp(log_a)
    u = jnp.sqrt(-jnp.expm1(2.0 * log_a)) * (i * xf)

    def combine(left, right):
        a1, b1 = left
        a2, b2 = right
        return a1 * a2, a2 * b1 + b2

    _, h = lax.associative_scan(combine, (a, u), axis=1)
    return h.astype(x.dtype)


def rotary(x, pos):
    half = RET_HD // 2
    inv = ROPE_BASE ** (-jnp.arange(half, dtype=F32) / half)
    ang = pos.astype(F32)[:, None] * inv[None, :]
    cos = jnp.cos(ang).astype(x.dtype)
    sin = jnp.sin(ang).astype(x.dtype)
    x1, x2 = x[..., :half], x[..., half:]
    return jnp.concatenate([x1 * cos - x2 * sin, x1 * sin + x2 * cos], axis=-1)


def chunkwise_retention(q, k, v):
    B, H, S, Dh = q.shape
    C = RET_CHUNK
    N = S // C
    dt = q.dtype
    log_g = jnp.log(1.0 - 2.0 ** (-5.0 - jnp.arange(H, dtype=F32)))
    idx = jnp.arange(C, dtype=F32)
    diff = idx[:, None] - idx[None, :]
    decay = jnp.where(diff >= 0, jnp.exp(jnp.maximum(diff, 0.0)[None] * log_g[:, None, None]), 0.0)
    qc = q.reshape(B, H, N, C, Dh)
    kc = k.reshape(B, H, N, C, Dh)
    vc = v.reshape(B, H, N, C, Dh)
    scores = jnp.einsum('bhncd,bhnmd->bhncm', qc, kc) * decay[None, :, None].astype(dt)
    inner = jnp.einsum('bhncm,bhnme->bhnce', scores, vc)
    k_decay = jnp.exp((C - 1.0 - idx)[None, :] * log_g[:, None]).astype(dt)
    chunk_kv = jnp.einsum('bhncd,bhnce->bhnde', kc * k_decay[None, :, None, :, None], vc)
    chunk_decay = jnp.exp(C * log_g).astype(dt)[None, :, None, None]

    def step(state, kv):
        return state * chunk_decay + kv, state

    init = jnp.zeros((B, H, Dh, Dh), chunk_kv.dtype)
    _, prev = lax.scan(step, init, jnp.moveaxis(chunk_kv, 2, 0))
    prev = jnp.moveaxis(prev, 0, 2)
    q_decay = jnp.exp((idx + 1.0)[None, :] * log_g[:, None]).astype(dt)
    cross = jnp.einsum('bhncd,bhnde->bhnce', qc * q_decay[None, :, None, :, None], prev)
    return (inner + cross).reshape(B, H, S, Dh)


def head_group_norm(o):
    of = o.astype(F32)
    mu = jnp.mean(of, axis=-1, keepdims=True)
    var = jnp.mean(jnp.square(of - mu), axis=-1, keepdims=True)
    return ((of - mu) * lax.rsqrt(var + EPS)).astype(o.dtype)


def hierarchical_route(h, w_grp, b_grp, w_exp, b_exp):
    T = h.shape[0]
    hf = h.astype(F32)
    g_logits = hf @ w_grp.astype(F32) + b_grp.astype(F32)
    g_prob = jax.nn.softmax(g_logits, axis=-1)
    g_sel = jnp.argmax(g_logits, axis=-1)
    g_w = jnp.max(g_prob, axis=-1)
    e_logits = (hf @ w_exp.astype(F32) + b_exp.astype(F32)).reshape(T, N_GROUPS, EXPERTS_PER_GROUP)
    sel = jnp.broadcast_to(g_sel[:, None, None], (T, 1, EXPERTS_PER_GROUP))
    e_in = jnp.take_along_axis(e_logits, sel, axis=1)[:, 0]
    top_v, top_i = lax.top_k(e_in, TOP_K)
    top_w = jax.nn.softmax(top_v, axis=-1) * g_w[:, None]
    expert_id = g_sel[:, None].astype(jnp.int32) * EXPERTS_PER_GROUP + top_i.astype(jnp.int32)
    return expert_id, top_w


def sparse_moe(h, expert_id, weight, w1, w3, w2):
    T, D = h.shape
    A = T * TOP_K
    e_flat = expert_id.reshape(A)
    tok_flat = jnp.repeat(jnp.arange(T, dtype=jnp.int32), TOP_K)
    w_flat = weight.reshape(A)
    order = jnp.argsort(e_flat)
    e_s, tok_s, w_s = e_flat[order], tok_flat[order], w_flat[order]
    counts = jnp.bincount(e_flat, length=N_EXPERTS)
    padded = (counts + MOE_BLOCK - 1) // MOE_BLOCK * MOE_BLOCK
    pad_end = jnp.cumsum(padded)
    pad_start = pad_end - padded
    start = jnp.cumsum(counts) - counts
    dest = pad_start[e_s] + (jnp.arange(A, dtype=jnp.int32) - start[e_s])
    n_blocks = -(-A // MOE_BLOCK) + N_EXPERTS
    n_slots = n_blocks * MOE_BLOCK
    slot_tok = jnp.zeros((n_slots,), jnp.int32).at[dest].set(tok_s)
    slot_w = jnp.zeros((n_slots,), w_s.dtype).at[dest].set(w_s)
    block_start = jnp.arange(n_blocks, dtype=pad_end.dtype) * MOE_BLOCK
    block_exp = jnp.minimum(jnp.searchsorted(pad_end, block_start, side='right'), N_EXPERTS - 1)
    xs = h[slot_tok].reshape(n_blocks, MOE_BLOCK, D)

    def expert_block(args):
        xb, e = args
        return (jax.nn.silu(xb @ w1[e]) * (xb @ w3[e])) @ w2[e]

    ys = lax.map(expert_block, (xs, block_exp)).reshape(n_slots, D)
    return jnp.zeros_like(h).at[slot_tok].add(ys * slot_w.astype(ys.dtype)[:, None])


def setup_inputs(seed: int = 0) -> dict:
    key = jax.random.key(seed)
    ks = jax.random.split(key, 32)
    L, D = DEPTH, D_MODEL
    nrm = lambda k, shape, s: jax.random.normal(k, shape, F32) * s
    u = jax.random.uniform(ks[10], (L, D_LRU), F32, 0.9, 0.999)
    sig = u ** (1.0 / LRU_C)
    lam = jnp.log(sig) - jnp.log1p(-sig)
    return {
        "x": nrm(ks[0], (BATCH, SEQ, D), 1.0),
        "p": nrm(ks[1], (L, BATCH, SEQ, PLE_DIM), 1.0),
        "g_mix": 1.0 + nrm(ks[2], (L, D), 0.01),
        "w_in": nrm(ks[3], (L, D, IN_COLS), D ** -0.5),
        "conv_w": nrm(ks[4], (L, CONV_W, D_LRU), CONV_W ** -0.5),
        "conv_b": nrm(ks[5], (L, D_LRU), 0.01),
        "lru_wa": nrm(ks[6], (L, LRU_BLOCKS, LRU_BW, LRU_BW), LRU_BW ** -0.5),
        "lru_ba": nrm(ks[7], (L, LRU_BLOCKS, LRU_BW), 0.01),
        "lru_wx": nrm(ks[8], (L, LRU_BLOCKS, LRU_BW, LRU_BW), LRU_BW ** -0.5),
        "lru_bx": nrm(ks[9], (L, LRU_BLOCKS, LRU_BW), 0.01),
        "lru_lambda": lam,
        "w_out": nrm(ks[11], (L, D_MIX, D), D_MIX ** -0.5),
        "g_ffn": 1.0 + nrm(ks[12], (L, D), 0.01),
        "w_router_group": nrm(ks[13], (L, D, N_GROUPS), D ** -0.5),
        "b_router_group": nrm(ks[14], (L, N_GROUPS), 0.01),
        "w_router_expert": nrm(ks[15], (L, D, N_EXPERTS), D ** -0.5),
        "b_router_expert": nrm(ks[16], (L, N_EXPERTS), 0.01),
        "w1": nrm(ks[17], (L, N_EXPERTS, D, D_EXPERT), D ** -0.5),
        "w3": nrm(ks[18], (L, N_EXPERTS, D, D_EXPERT), D ** -0.5),
        "w2": nrm(ks[19], (L, N_EXPERTS, D_EXPERT, D), D_EXPERT ** -0.5),
        "g_ple": 1.0 + nrm(ks[20], (L, D), 0.01),
        "w_ple_gate": nrm(ks[21], (L, D, D), D ** -0.5),
        "b_ple_gate": nrm(ks[22], (L, D), 0.01),
        "w_ple_proj": nrm(ks[23], (L, PLE_DIM, D), PLE_DIM ** -0.5),
        "g_final": 1.0 + nrm(ks[24], (D,), 0.01),
    }


def reference(x, p, g_mix, w_in, conv_w, conv_b, lru_wa, lru_ba, lru_wx, lru_bx, lru_lambda, w_out,
              g_ffn, w_router_group, b_router_group, w_router_expert, b_router_expert, w1, w3, w2,
              g_ple, w_ple_gate, b_ple_gate, w_ple_proj, g_final):
    B, S, D = x.shape
    pos = jnp.arange(S, dtype=jnp.int32)
    for l in range(DEPTH):
        h = rms_norm(x, g_mix[l])
        proj = h @ w_in[l]
        xl, gl, q, k, v, gr = jnp.split(proj, SPLITS, axis=-1)
        xl = causal_depthwise_conv(xl, conv_w[l], conv_b[l])
        y_lru = rg_lru(xl, lru_wa[l], lru_ba[l], lru_wx[l], lru_bx[l], lru_lambda[l]) * jax.nn.gelu(gl)
        to_heads = lambda t: t.reshape(B, S, RET_HEADS, RET_HD).transpose(0, 2, 1, 3)
        qh = rotary(to_heads(q), pos)
        kh = rotary(to_heads(k), pos) * (RET_HD ** -0.5)
        o = chunkwise_retention(qh, kh, to_heads(v))
        o = head_group_norm(o).transpose(0, 2, 1, 3).reshape(B, S, D_RET)
        y_ret = jax.nn.silu(gr) * o
        x = x + jnp.concatenate([y_lru, y_ret], axis=-1) @ w_out[l]
        hf = rms_norm(x, g_ffn[l]).reshape(B * S, D)
        eid, ew = hierarchical_route(hf, w_router_group[l], b_router_group[l], w_router_expert[l], b_router_expert[l])
        x = x + sparse_moe(hf, eid, ew, w1[l], w3[l], w2[l]).reshape(B, S, D)
        gate = jax.nn.sigmoid(rms_norm(x, g_ple[l]) @ w_ple_gate[l] + b_ple_gate[l])
        x = x + gate * (p[l] @ w_ple_proj[l])
    return rms_norm(x, g_final)
```

```python
import functools

import jax
import jax.numpy as jnp
from jax import lax
from jax.experimental import pallas as pl
from jax.experimental.pallas import tpu as pltpu

F32 = jnp.float32
BF16 = jnp.bfloat16
U32 = jnp.uint32
I32 = jnp.int32

EPS = 1e-6
LRU_C = 8.0
CONV_W = 4
RET_HEADS = 4
ROPE_BASE = 10000.0
N_GROUPS = 8
EXPERTS_PER_GROUP = 8
N_EXPERTS = N_GROUPS * EXPERTS_PER_GROUP

LANES = 128
SUBLANES = 8
VMEM_LIMIT = 56 * 1024 * 1024

MIX_TILE = 512
RET_CHUNK = 256
ROUTE_TILE = 512
DISPATCH_TILE = 512
EXPERT_BLOCK = 256
COMBINE_TILE = 256


def _rms(x, g):
    ms = jnp.mean(x * x, axis=-1, keepdims=True)
    return (x * lax.rsqrt(ms + EPS)) * g


def _sigmoid(z):
    return 1.0 / (1.0 + jnp.exp(-z))


def _pack_bf16_pair(a, b):
    ab = lax.bitcast_convert_type(a.astype(BF16).astype(F32), U32)
    bb = lax.bitcast_convert_type(b.astype(BF16).astype(F32), U32)
    return ab | (bb >> 16)


def _unpack_bf16_pair(w):
    hi = lax.bitcast_convert_type(w & jnp.uint32(0xFFFF0000), F32)
    lo = lax.bitcast_convert_type(w << 16, F32)
    return hi, lo


def _shift_rows(x, d, fill, row):
    return jnp.where(row >= d, pltpu.roll(x, d, 0), fill)


def _linear_scan(a, u):
    n = a.shape[0]
    row = lax.broadcasted_iota(I32, a.shape, 0)
    d = 1
    while d < n:
        a_sh = _shift_rows(a, d, 1.0, row)
        u_sh = _shift_rows(u, d, 0.0, row)
        u = u + a * u_sh
        a = a * a_sh
        d *= 2
    return u


def _mix_kernel(x_ref, gmix_ref, win_ref, convw_ref, convb_ref, wg_ref, bg_ref, lam_ref,
                cos_ref, sin_ref, dmask_ref, qdec_ref, kdec_ref, cdec_ref, wout_ref,
                o_ref, xl_ext, hcar, state, ybuf):
    ts = x_ref.shape[1]
    d_lru = lam_ref.shape[1]
    d_ret = qdec_ref.shape[1]
    hd = d_ret // RET_HEADS
    chunk = dmask_ref.shape[1]
    half = d_lru // 2

    @pl.when(pl.program_id(1) == 0)
    def _():
        xl_ext[0:SUBLANES, :] = jnp.zeros((SUBLANES, d_lru), F32)
        hcar[...] = jnp.zeros_like(hcar)
        state[...] = jnp.zeros_like(state)

    x = x_ref[0]
    h = _rms(x, gmix_ref[...]).astype(BF16)

    def proj(lo, width):
        return jnp.dot(h, win_ref[:, lo:lo + width], preferred_element_type=F32)

    xl = proj(0, d_lru)
    xl_ext[SUBLANES:SUBLANES + ts, :] = xl
    xc = convb_ref[...] + xl_ext[SUBLANES - 3:SUBLANES - 3 + ts, :] * convw_ref[0:1, :]
    for k in range(1, CONV_W):
        off = SUBLANES - (CONV_W - 1) + k
        xc = xc + xl_ext[off:off + ts, :] * convw_ref[k:k + 1, :]
    xl_ext[0:SUBLANES, :] = xl_ext[ts:ts + SUBLANES, :]

    xcb = xc.astype(BF16)
    g0 = jnp.dot(xcb[:, :half], wg_ref[0], preferred_element_type=F32)
    g1 = jnp.dot(xcb[:, half:], wg_ref[1], preferred_element_type=F32)
    ra = jnp.concatenate([g0[:, :half], g1[:, :half]], axis=1) + bg_ref[:, :d_lru]
    ix = jnp.concatenate([g0[:, half:], g1[:, half:]], axis=1) + bg_ref[:, d_lru:]
    r = _sigmoid(ra)
    i_gate = _sigmoid(ix)
    z = -lam_ref[...]
    softplus = jnp.maximum(z, 0.0) + jnp.log1p(jnp.exp(-jnp.abs(z)))
    log_a = (-LRU_C * r) * softplus
    a = jnp.exp(log_a)
    u = jnp.sqrt(1.0 - a * a) * (i_gate * xc)
    row = lax.broadcasted_iota(I32, a.shape, 0)
    u = u + jnp.where(row == 0, a * hcar[0:1, :], 0.0)
    hseq = _linear_scan(a, u)
    hcar[0:1, :] = hseq[ts - 1:ts, :]
    gl = proj(d_lru, d_lru)
    ybuf[:, 0:d_lru] = (hseq * jax.nn.gelu(gl)).astype(BF16)

    base = 2 * d_lru
    q = proj(base, d_ret)
    k = proj(base + d_ret, d_ret)
    v = proj(base + 2 * d_ret, d_ret)
    gr = proj(base + 3 * d_ret, d_ret)
    scale = hd ** -0.5
    for hh in range(RET_HEADS):
        ls = slice(hh * hd, (hh + 1) * hd)
        for c in range(ts // chunk):
            rs = slice(c * chunk, (c + 1) * chunk)
            cosv = cos_ref[rs, :]
            sinv = sin_ref[rs, :]
            qh = q[rs, ls]
            kh = k[rs, ls]
            qr = qh * cosv + pltpu.roll(qh, hd // 2, 1) * sinv
            kr = (kh * cosv + pltpu.roll(kh, hd // 2, 1) * sinv) * scale
            vb = v[rs, ls].astype(BF16)
            scores = lax.dot_general(qr.astype(BF16), kr.astype(BF16), (((1,), (1,)), ((), ())),
                                     preferred_element_type=F32)
            inner = jnp.dot((scores * dmask_ref[hh]).astype(BF16), vb, preferred_element_type=F32)
            st = state[hh]
            cross = jnp.dot((qr * qdec_ref[:, ls]).astype(BF16), st.astype(BF16), preferred_element_type=F32)
            kv = lax.dot_general((kr * kdec_ref[:, ls]).astype(BF16), vb, (((0,), (0,)), ((), ())),
                                 preferred_element_type=F32)
            state[hh] = st * cdec_ref[:, ls] + kv
            o = inner + cross
            mu = jnp.mean(o, axis=-1, keepdims=True)
            oc = o - mu
            var = jnp.mean(oc * oc, axis=-1, keepdims=True)
            on = oc * lax.rsqrt(var + EPS)
            grh = gr[rs, ls]
            ybuf[rs, d_lru + hh * hd:d_lru + (hh + 1) * hd] = ((grh * _sigmoid(grh)) * on).astype(BF16)

    o_ref[0] = x + jnp.dot(ybuf[...], wout_ref[...], preferred_element_type=F32)


def _mix(x, g_mix, w_in, conv_w, conv_b, lru_wa, lru_ba, lru_wx, lru_bx, lam, w_out):
    B, S, D = x.shape
    d_lru = lam.shape[0]
    d_ret = (w_in.shape[1] - 2 * d_lru) // 4
    hd = d_ret // RET_HEADS
    ts, chunk = MIX_TILE, RET_CHUNK
    assert S % ts == 0 and ts % chunk == 0 and d_lru % (2 * LANES) == 0 and hd == LANES
    nb, bw = lru_wa.shape[0], lru_wa.shape[1]
    half = d_lru // 2
    per_half = nb // 2

    def blockdiag(w):
        out = jnp.zeros((half, half), F32)
        for j in range(per_half):
            out = out.at[j * bw:(j + 1) * bw, j * bw:(j + 1) * bw].set(w[j])
        return out
    wg = jnp.stack([jnp.concatenate([blockdiag(lru_wa[hf * per_half:(hf + 1) * per_half]),
                                     blockdiag(lru_wx[hf * per_half:(hf + 1) * per_half])], axis=1)
                    for hf in range(2)]).astype(BF16)
    bg = jnp.concatenate([lru_ba.reshape(1, d_lru), lru_bx.reshape(1, d_lru)], axis=1)

    inv = ROPE_BASE ** (-jnp.arange(hd // 2, dtype=F32) / (hd // 2))
    ang = jnp.arange(S, dtype=F32)[:, None] * inv[None, :]
    cos2 = jnp.concatenate([jnp.cos(ang), jnp.cos(ang)], axis=1)
    sin2 = jnp.concatenate([-jnp.sin(ang), jnp.sin(ang)], axis=1)
    log_g = jnp.log(1.0 - 2.0 ** (-5.0 - jnp.arange(RET_HEADS, dtype=F32)))
    idx = jnp.arange(chunk, dtype=F32)
    diff = idx[:, None] - idx[None, :]
    dmask = jnp.where(diff >= 0, jnp.exp(jnp.maximum(diff, 0.0)[None] * log_g[:, None, None]), 0.0)
    rep = lambda t: jnp.repeat(t, hd, axis=-1)
    qdec = rep(jnp.exp((idx + 1.0)[:, None] * log_g[None, :]))
    kdec = rep(jnp.exp((chunk - 1.0 - idx)[:, None] * log_g[None, :]))
    cdec = rep(jnp.exp(chunk * log_g)[None, :])

    full = lambda shape: pl.BlockSpec(shape, lambda b, s: (0,) * len(shape))
    return pl.pallas_call(
        _mix_kernel,
        name="mix",
        grid=(B, S // ts),
        in_specs=[
            pl.BlockSpec((1, ts, D), lambda b, s: (b, s, 0)),
            full((1, D)),
            full(w_in.shape),
            full((CONV_W, d_lru)),
            full((1, d_lru)),
            full(wg.shape),
            full(bg.shape),
            full((1, d_lru)),
            pl.BlockSpec((ts, hd), lambda b, s: (s, 0)),
            pl.BlockSpec((ts, hd), lambda b, s: (s, 0)),
            full(dmask.shape),
            full(qdec.shape),
            full(kdec.shape),
            full(cdec.shape),
            full(w_out.shape),
        ],
        out_specs=pl.BlockSpec((1, ts, D), lambda b, s: (b, s, 0)),
        out_shape=jax.ShapeDtypeStruct((B, S, D), F32),
        scratch_shapes=[
            pltpu.VMEM((ts + 2 * SUBLANES, d_lru), F32),
            pltpu.VMEM((SUBLANES, d_lru), F32),
            pltpu.VMEM((RET_HEADS, hd, hd), F32),
            pltpu.VMEM((ts, d_lru + d_ret), BF16),
        ],
        compiler_params=pltpu.CompilerParams(
            dimension_semantics=("arbitrary", "arbitrary"), vmem_limit_bytes=VMEM_LIMIT),
    )(x, g_mix.reshape(1, D), w_in.astype(BF16), conv_w, conv_b.reshape(1, d_lru), wg, bg,
      lam.reshape(1, d_lru), cos2, sin2, dmask, qdec, kdec, cdec, w_out.astype(BF16))


def _route_kernel(x_ref, g_ref, wr_ref, br_ref, hfp_ref, info_ref, cnt_ref, carry):
    tm, D = x_ref.shape

    @pl.when(pl.program_id(0) == 0)
    def _():
        carry[...] = jnp.zeros_like(carry)

    hf = _rms(x_ref[...], g_ref[...])
    hfp_ref[...] = _pack_bf16_pair(hf[:, :D // 2], hf[:, D // 2:])

    logits = jnp.dot(hf, wr_ref[...], preferred_element_type=F32, precision=lax.Precision.HIGHEST) + br_ref[...]
    lane = lax.broadcasted_iota(I32, logits.shape, 1)
    neg = jnp.float32(-jnp.inf)
    gmask = lane < N_GROUPS
    gl = jnp.where(gmask, logits, neg)
    gmax = jnp.max(gl, axis=-1, keepdims=True)
    g_sel = jnp.min(jnp.where(gl == gmax, lane, LANES), axis=-1, keepdims=True)
    g_w = 1.0 / jnp.sum(jnp.where(gmask, jnp.exp(logits - gmax), 0.0), axis=-1, keepdims=True)

    e_lane = lane - N_GROUPS
    emask = (e_lane >= 0) & (e_lane < N_EXPERTS) & ((e_lane >> 3) == g_sel)
    el = jnp.where(emask, logits, neg)
    v1 = jnp.max(el, axis=-1, keepdims=True)
    i1 = jnp.min(jnp.where(el == v1, e_lane, LANES), axis=-1, keepdims=True)
    el2 = jnp.where(e_lane == i1, neg, el)
    v2 = jnp.max(el2, axis=-1, keepdims=True)
    i2 = jnp.min(jnp.where(el2 == v2, e_lane, LANES), axis=-1, keepdims=True)
    t = jnp.exp(v2 - v1)
    w1 = g_w / (1.0 + t)
    w2 = g_w * (t / (1.0 + t))

    hit = (lane == i1) | (lane == i2)
    onehot = jnp.where(hit, 1.0, 0.0)
    rr = lax.broadcasted_iota(I32, (tm, tm), 0)
    cc = lax.broadcasted_iota(I32, (tm, tm), 1)
    tri = jnp.where(rr > cc, 1.0, 0.0).astype(BF16)
    prefix = jnp.dot(tri, onehot.astype(BF16), preferred_element_type=F32) + carry[0:1, :]
    r1 = jnp.sum(jnp.where(lane == i1, prefix, 0.0), axis=-1, keepdims=True)
    r2 = jnp.sum(jnp.where(lane == i2, prefix, 0.0), axis=-1, keepdims=True)
    carry[0:1, :] = carry[0:1, :] + jnp.sum(onehot, axis=0, keepdims=True)

    info = jnp.where(lane == 0, i1.astype(F32), 0.0)
    info = jnp.where(lane == 1, i2.astype(F32), info)
    info = jnp.where(lane == 2, w1, info)
    info = jnp.where(lane == 3, w2, info)
    info = jnp.where(lane == 4, r1, info)
    info = jnp.where(lane == 5, r2, info)
    info_ref[...] = info
    cnt_ref[...] = carry[...]


def _route(x1, g_ffn, w_grp, b_grp, w_exp, b_exp):
    T, D = x1.shape
    tm = ROUTE_TILE
    assert T % tm == 0 and N_GROUPS + N_EXPERTS <= LANES and T < (1 << 24)
    pad = LANES - N_GROUPS - N_EXPERTS
    wr = jnp.concatenate([w_grp, w_exp, jnp.zeros((D, pad), F32)], axis=1)
    br = jnp.concatenate([b_grp, b_exp, jnp.zeros((pad,), F32)]).reshape(1, LANES)
    return pl.pallas_call(
        _route_kernel,
        name="route",
        grid=(T // tm,),
        in_specs=[
            pl.BlockSpec((tm, D), lambda i: (i, 0)),
            pl.BlockSpec((1, D), lambda i: (0, 0)),
            pl.BlockSpec((D, LANES), lambda i: (0, 0)),
            pl.BlockSpec((1, LANES), lambda i: (0, 0)),
        ],
        out_specs=[
            pl.BlockSpec((tm, D // 2), lambda i: (i, 0)),
            pl.BlockSpec((tm, LANES), lambda i: (i, 0)),
            pl.BlockSpec((SUBLANES, LANES), lambda i: (0, 0)),
        ],
        out_shape=[
            jax.ShapeDtypeStruct((T, D // 2), U32),
            jax.ShapeDtypeStruct((T, LANES), F32),
            jax.ShapeDtypeStruct((SUBLANES, LANES), F32),
        ],
        scratch_shapes=[pltpu.VMEM((SUBLANES, LANES), F32)],
        compiler_params=pltpu.CompilerParams(
            dimension_semantics=("arbitrary",), vmem_limit_bytes=VMEM_LIMIT),
    )(x1, g_ffn.reshape(1, D), wr, br)


def _row_copy(src_ref, src_row, dst_ref, dst_row, sem):
    return pltpu.make_async_copy(src_ref.at[pl.ds(src_row, 1), :], dst_ref.at[pl.ds(dst_row, 1), :], sem)


def _dispatch_kernel(dest_ref, hfp_ref, xs_in_ref, xs_ref, sem):
    del xs_in_ref
    tm = hfp_ref.shape[0]

    def start(t, c):
        _row_copy(hfp_ref, t, xs_ref, dest_ref[2 * t], sem).start()
        _row_copy(hfp_ref, t, xs_ref, dest_ref[2 * t + 1], sem).start()
        return c

    lax.fori_loop(0, tm, start, 0)

    def wait(t, c):
        _row_copy(hfp_ref, t, xs_ref, dest_ref[2 * t], sem).wait()
        _row_copy(hfp_ref, t, xs_ref, dest_ref[2 * t + 1], sem).wait()
        return c

    lax.fori_loop(0, tm, wait, 0)


def _dispatch(dest, hfp, n_slots):
    T, W = hfp.shape
    tm = DISPATCH_TILE
    assert T % tm == 0
    xs0 = jnp.zeros((n_slots, W), U32)
    return pl.pallas_call(
        _dispatch_kernel,
        name="dispatch",
        grid=(T // tm,),
        in_specs=[
            pl.BlockSpec((2 * tm,), lambda i: (i,), memory_space=pltpu.SMEM),
            pl.BlockSpec((tm, W), lambda i: (i, 0)),
            pl.BlockSpec(memory_space=pl.ANY),
        ],
        out_specs=pl.BlockSpec(memory_space=pl.ANY),
        out_shape=jax.ShapeDtypeStruct((n_slots, W), U32),
        scratch_shapes=[pltpu.SemaphoreType.DMA],
        input_output_aliases={2: 0},
        compiler_params=pltpu.CompilerParams(
            dimension_semantics=("arbitrary",), has_side_effects=True, vmem_limit_bytes=VMEM_LIMIT),
    )(dest, hfp, xs0)


def _experts_kernel(bexp_ref, nvalid_ref, xs_ref, w1_ref, w3_ref, w2_ref, ys_ref, w1b, w3b, w2b):
    i = pl.program_id(0)
    half = xs_ref.shape[1]
    prev = bexp_ref[jnp.maximum(i - 1, 0)]
    valid = i < nvalid_ref[0]

    @pl.when(valid & ((i == 0) | (bexp_ref[i] != prev)))
    def _():
        w1b[...] = w1_ref[0].astype(BF16)
        w3b[...] = w3_ref[0].astype(BF16)
        w2b[...] = w2_ref[0].astype(BF16)

    @pl.when(valid)
    def _():
        hi, lo = _unpack_bf16_pair(xs_ref[...])
        hi = hi.astype(BF16)
        lo = lo.astype(BF16)
        a1 = (jnp.dot(hi, w1b[:half, :], preferred_element_type=F32)
              + jnp.dot(lo, w1b[half:, :], preferred_element_type=F32))
        a3 = (jnp.dot(hi, w3b[:half, :], preferred_element_type=F32)
              + jnp.dot(lo, w3b[half:, :], preferred_element_type=F32))
        hmid = ((a1 * _sigmoid(a1)) * a3).astype(BF16)
        y = jnp.dot(hmid, w2b[...], preferred_element_type=F32)
        ys_ref[...] = _pack_bf16_pair(y[:, :half], y[:, half:])


def _experts(block_exp, n_valid, xs, w1, w3, w2):
    n_slots, W = xs.shape
    bm = EXPERT_BLOCK
    n_blocks = n_slots // bm
    E, D, DE = w1.shape
    assert D == 2 * W
    clamp = lambda i, be, nv: jnp.minimum(i, nv[0] - 1)
    grid_spec = pltpu.PrefetchScalarGridSpec(
        num_scalar_prefetch=2,
        grid=(n_blocks,),
        in_specs=[
            pl.BlockSpec((bm, W), lambda i, be, nv: (clamp(i, be, nv), 0)),
            pl.BlockSpec((1, D, DE), lambda i, be, nv: (be[clamp(i, be, nv)], 0, 0)),
            pl.BlockSpec((1, D, DE), lambda i, be, nv: (be[clamp(i, be, nv)], 0, 0)),
            pl.BlockSpec((1, DE, D), lambda i, be, nv: (be[clamp(i, be, nv)], 0, 0)),
        ],
        out_specs=pl.BlockSpec((bm, W), lambda i, be, nv: (clamp(i, be, nv), 0)),
        scratch_shapes=[
            pltpu.VMEM((D, DE), BF16),
            pltpu.VMEM((D, DE), BF16),
            pltpu.VMEM((DE, D), BF16),
        ],
    )
    return pl.pallas_call(
        _experts_kernel,
        name="experts",
        grid_spec=grid_spec,
        out_shape=jax.ShapeDtypeStruct((n_slots, W), U32),
        input_output_aliases={2: 0},
        compiler_params=pltpu.CompilerParams(
            dimension_semantics=("arbitrary",), vmem_limit_bytes=VMEM_LIMIT),
    )(block_exp, n_valid, xs, w1, w3, w2)


def _combine_kernel(dest_ref, x1_ref, p_ref, info_ref, ys_ref, gple_ref, wgate_ref, bgate_ref,
                    wproj_ref, gfin_ref, o_ref, gbuf, sem, *, final_norm):
    tm = x1_ref.shape[0]

    def start(t, c):
        _row_copy(ys_ref, dest_ref[2 * t], gbuf.at[0], t, sem).start()
        _row_copy(ys_ref, dest_ref[2 * t + 1], gbuf.at[1], t, sem).start()
        return c

    lax.fori_loop(0, tm, start, 0)
    pp = jnp.dot(p_ref[...].astype(BF16), wproj_ref[...], preferred_element_type=F32)

    def wait(t, c):
        _row_copy(ys_ref, dest_ref[2 * t], gbuf.at[0], t, sem).wait()
        _row_copy(ys_ref, dest_ref[2 * t + 1], gbuf.at[1], t, sem).wait()
        return c

    lax.fori_loop(0, tm, wait, 0)

    info = info_ref[...]
    w_a = info[:, 2:3]
    w_b = info[:, 3:4]
    a_hi, a_lo = _unpack_bf16_pair(gbuf[0])
    b_hi, b_lo = _unpack_bf16_pair(gbuf[1])
    moe = jnp.concatenate([a_hi * w_a + b_hi * w_b, a_lo * w_a + b_lo * w_b], axis=1)
    x2 = x1_ref[...] + moe
    gate = _sigmoid(jnp.dot(_rms(x2, gple_ref[...]).astype(BF16), wgate_ref[...], preferred_element_type=F32)
                    + bgate_ref[...])
    x3 = x2 + gate * pp
    o_ref[...] = _rms(x3, gfin_ref[...]) if final_norm else x3


def _combine(dest, x1, p, info, ys, g_ple, w_gate, b_gate, w_proj, g_final, final_norm):
    T, D = x1.shape
    W = ys.shape[1]
    PD = p.shape[1]
    tm = COMBINE_TILE
    assert T % tm == 0
    const = lambda shape: pl.BlockSpec(shape, lambda i: (0,) * len(shape))
    return pl.pallas_call(
        functools.partial(_combine_kernel, final_norm=final_norm),
        name="combine",
        grid=(T // tm,),
        in_specs=[
            pl.BlockSpec((2 * tm,), lambda i: (i,), memory_space=pltpu.SMEM),
            pl.BlockSpec((tm, D), lambda i: (i, 0)),
            pl.BlockSpec((tm, PD), lambda i: (i, 0)),
            pl.BlockSpec((tm, LANES), lambda i: (i, 0)),
            pl.BlockSpec(memory_space=pl.ANY),
            const((1, D)),
            const((D, D)),
            const((1, D)),
            const((PD, D)),
            const((1, D)),
        ],
        out_specs=pl.BlockSpec((tm, D), lambda i: (i, 0)),
        out_shape=jax.ShapeDtypeStruct((T, D), F32),
        scratch_shapes=[pltpu.VMEM((2, tm, W), U32), pltpu.SemaphoreType.DMA],
        compiler_params=pltpu.CompilerParams(
            dimension_semantics=("arbitrary",), vmem_limit_bytes=VMEM_LIMIT),
    )(dest, x1, p, info, ys, g_ple.reshape(1, D), w_gate.astype(BF16), b_gate.reshape(1, D),
      w_proj.astype(BF16), g_final.reshape(1, D))


def _slot_plan(info, counts_f, n_assign):
    bm = EXPERT_BLOCK
    counts = counts_f[0, :N_EXPERTS].astype(I32)
    padded = (counts + bm - 1) // bm * bm
    pad_end = jnp.cumsum(padded)
    pad_start = pad_end - padded
    n_blocks = -(-n_assign // bm) + N_EXPERTS
    block_start = jnp.arange(n_blocks, dtype=I32) * bm
    block_exp = jnp.minimum(jnp.searchsorted(pad_end, block_start, side='right'), N_EXPERTS - 1).astype(I32)
    n_valid = (pad_end[-1] // bm).astype(I32).reshape(1)
    e = info[:, 0:2].astype(I32)
    rank = info[:, 4:6].astype(I32)
    dest = (pad_start[e] + rank).reshape(-1)
    return dest, block_exp, n_valid, n_blocks * bm


def kernel(x, p, g_mix, w_in, conv_w, conv_b, lru_wa, lru_ba, lru_wx, lru_bx, lru_lambda, w_out, g_ffn, w_router_group, b_router_group, w_router_expert, b_router_expert, w1, w3, w2, g_ple, w_ple_gate, b_ple_gate, w_ple_proj, g_final):
    B, S, D = x.shape
    T = B * S
    depth = g_mix.shape[0]
    for l in range(depth):
        x1 = _mix(x, g_mix[l], w_in[l], conv_w[l], conv_b[l], lru_wa[l], lru_ba[l], lru_wx[l], lru_bx[l],
                  lru_lambda[l], w_out[l]).reshape(T, D)
        hfp, info, counts = _route(x1, g_ffn[l], w_router_group[l], b_router_group[l],
                                   w_router_expert[l], b_router_expert[l])
        dest, block_exp, n_valid, n_slots = _slot_plan(info, counts, 2 * T)
        xs = _dispatch(dest, hfp, n_slots)
        ys = _experts(block_exp, n_valid, xs, w1[l], w3[l], w2[l])
        x = _combine(dest, x1, p[l].reshape(T, -1), info, ys, g_ple[l], w_ple_gate[l], b_ple_gate[l],
                     w_ple_proj[l], g_final, l == depth - 1).reshape(B, S, D)
    return x
```

```python
import functools

import jax
import jax.numpy as jnp
from jax import lax
from jax.experimental import pallas as pl
from jax.experimental.pallas import tpu as pltpu

F32 = jnp.float32
BF16 = jnp.bfloat16
U32 = jnp.uint32
I32 = jnp.int32

EPS = 1e-6
LRU_C = 8.0
CONV_W = 4
RET_HEADS = 4
ROPE_BASE = 10000.0
N_GROUPS = 8
EXPERTS_PER_GROUP = 8
N_EXPERTS = N_GROUPS * EXPERTS_PER_GROUP

LANES = 128
SUBLANES = 8
VMEM_LIMIT = 56 * 1024 * 1024

MIX_TILE = 512
RET_CHUNK = 256
ROUTE_TILE = 512
DISPATCH_TILE = 512
EXPERT_BLOCK = 256
COMBINE_TILE = 256
DMA_UNROLL = 8


def _rms(x, g):
    ms = jnp.mean(x * x, axis=-1, keepdims=True)
    return (x * lax.rsqrt(ms + EPS)) * g


def _sigmoid(z):
    return 1.0 / (1.0 + jnp.exp(-z))


def _pack_bf16_pair(a, b):
    ab = lax.bitcast_convert_type(a.astype(BF16).astype(F32), U32)
    bb = lax.bitcast_convert_type(b.astype(BF16).astype(F32), U32)
    return ab | (bb >> 16)


def _unpack_bf16_pair(w):
    hi = lax.bitcast_convert_type(w & jnp.uint32(0xFFFF0000), F32)
    lo = lax.bitcast_convert_type(w << 16, F32)
    return hi, lo


def _shift_rows(x, d, fill, row):
    return jnp.where(row >= d, pltpu.roll(x, d, 0), fill)


def _linear_scan(a, u):
    n = a.shape[0]
    row = lax.broadcasted_iota(I32, a.shape, 0)
    d = 1
    while d < n:
        a_sh = _shift_rows(a, d, 1.0, row)
        u_sh = _shift_rows(u, d, 0.0, row)
        u = u + a * u_sh
        a = a * a_sh
        d *= 2
    return u


def _mix_kernel(x_ref, gmix_ref, win_ref, convw_ref, convb_ref, wg_ref, bg_ref, lam_ref,
                cos_ref, sin_ref, dmask_ref, qdec_ref, kdec_ref, cdec_ref, wout_ref,
                o_ref, xl_ext, hcar, state, ybuf):
    ts = x_ref.shape[1]
    d_lru = lam_ref.shape[1]
    d_ret = qdec_ref.shape[1]
    hd = d_ret // RET_HEADS
    chunk = dmask_ref.shape[1]
    half = d_lru // 2

    @pl.when(pl.program_id(1) == 0)
    def _():
        xl_ext[0:SUBLANES, :] = jnp.zeros((SUBLANES, d_lru), F32)
        hcar[...] = jnp.zeros_like(hcar)
        state[...] = jnp.zeros_like(state)

    x = x_ref[0]
    h = _rms(x, gmix_ref[...]).astype(BF16)

    def proj(lo, width):
        return jnp.dot(h, win_ref[:, lo:lo + width], preferred_element_type=F32)

    xl = proj(0, d_lru)
    xl_ext[SUBLANES:SUBLANES + ts, :] = xl
    xc = convb_ref[...] + xl_ext[SUBLANES - 3:SUBLANES - 3 + ts, :] * convw_ref[0:1, :]
    for k in range(1, CONV_W):
        off = SUBLANES - (CONV_W - 1) + k
        xc = xc + xl_ext[off:off + ts, :] * convw_ref[k:k + 1, :]
    xl_ext[0:SUBLANES, :] = xl_ext[ts:ts + SUBLANES, :]

    xcb = xc.astype(BF16)
    g0 = jnp.dot(xcb[:, :half], wg_ref[0], preferred_element_type=F32)
    g1 = jnp.dot(xcb[:, half:], wg_ref[1], preferred_element_type=F32)
    ra = jnp.concatenate([g0[:, :half], g1[:, :half]], axis=1) + bg_ref[:, :d_lru]
    ix = jnp.concatenate([g0[:, half:], g1[:, half:]], axis=1) + bg_ref[:, d_lru:]
    r = _sigmoid(ra)
    i_gate = _sigmoid(ix)
    z = -lam_ref[...]
    softplus = jnp.maximum(z, 0.0) + jnp.log1p(jnp.exp(-jnp.abs(z)))
    log_a = (-LRU_C * r) * softplus
    a = jnp.exp(log_a)
    u = jnp.sqrt(1.0 - a * a) * (i_gate * xc)
    row = lax.broadcasted_iota(I32, a.shape, 0)
    u = u + jnp.where(row == 0, a * hcar[0:1, :], 0.0)
    hseq = _linear_scan(a, u)
    hcar[0:1, :] = hseq[ts - 1:ts, :]
    gl = proj(d_lru, d_lru)
    ybuf[:, 0:d_lru] = (hseq * jax.nn.gelu(gl)).astype(BF16)

    base = 2 * d_lru
    q = proj(base, d_ret)
    k = proj(base + d_ret, d_ret)
    v = proj(base + 2 * d_ret, d_ret)
    gr = proj(base + 3 * d_ret, d_ret)
    scale = hd ** -0.5
    for hh in range(RET_HEADS):
        ls = slice(hh * hd, (hh + 1) * hd)
        for c in range(ts // chunk):
            rs = slice(c * chunk, (c + 1) * chunk)
            cosv = cos_ref[rs, :]
            sinv = sin_ref[rs, :]
            qh = q[rs, ls]
            kh = k[rs, ls]
            qr = qh * cosv + pltpu.roll(qh, hd // 2, 1) * sinv
            kr = (kh * cosv + pltpu.roll(kh, hd // 2, 1) * sinv) * scale
            vb = v[rs, ls].astype(BF16)
            scores = lax.dot_general(qr.astype(BF16), kr.astype(BF16), (((1,), (1,)), ((), ())),
                                     preferred_element_type=F32)
            inner = jnp.dot((scores * dmask_ref[hh]).astype(BF16), vb, preferred_element_type=F32)
            st = state[hh]
            cross = jnp.dot((qr * qdec_ref[:, ls]).astype(BF16), st.astype(BF16), preferred_element_type=F32)
            kv = lax.dot_general((kr * kdec_ref[:, ls]).astype(BF16), vb, (((0,), (0,)), ((), ())),
                                 preferred_element_type=F32)
            state[hh] = st * cdec_ref[:, ls] + kv
            o = inner + cross
            mu = jnp.mean(o, axis=-1, keepdims=True)
            oc = o - mu
            var = jnp.mean(oc * oc, axis=-1, keepdims=True)
            on = oc * lax.rsqrt(var + EPS)
            grh = gr[rs, ls]
            ybuf[rs, d_lru + hh * hd:d_lru + (hh + 1) * hd] = ((grh * _sigmoid(grh)) * on).astype(BF16)

    o_ref[0] = x + jnp.dot(ybuf[...], wout_ref[...], preferred_element_type=F32)


def _mix(x, g_mix, w_in, conv_w, conv_b, lru_wa, lru_ba, lru_wx, lru_bx, lam, w_out):
    B, S, D = x.shape
    d_lru = lam.shape[0]
    d_ret = (w_in.shape[1] - 2 * d_lru) // 4
    hd = d_ret // RET_HEADS
    ts, chunk = MIX_TILE, RET_CHUNK
    assert S % ts == 0 and ts % chunk == 0 and d_lru % (2 * LANES) == 0 and hd == LANES
    nb, bw = lru_wa.shape[0], lru_wa.shape[1]
    half = d_lru // 2
    per_half = nb // 2

    def blockdiag(w):
        out = jnp.zeros((half, half), F32)
        for j in range(per_half):
            out = out.at[j * bw:(j + 1) * bw, j * bw:(j + 1) * bw].set(w[j])
        return out
    wg = jnp.stack([jnp.concatenate([blockdiag(lru_wa[hf * per_half:(hf + 1) * per_half]),
                                     blockdiag(lru_wx[hf * per_half:(hf + 1) * per_half])], axis=1)
                    for hf in range(2)]).astype(BF16)
    bg = jnp.concatenate([lru_ba.reshape(1, d_lru), lru_bx.reshape(1, d_lru)], axis=1)

    inv = ROPE_BASE ** (-jnp.arange(hd // 2, dtype=F32) / (hd // 2))
    ang = jnp.arange(S, dtype=F32)[:, None] * inv[None, :]
    cos2 = jnp.concatenate([jnp.cos(ang), jnp.cos(ang)], axis=1)
    sin2 = jnp.concatenate([-jnp.sin(ang), jnp.sin(ang)], axis=1)
    log_g = jnp.log(1.0 - 2.0 ** (-5.0 - jnp.arange(RET_HEADS, dtype=F32)))
    idx = jnp.arange(chunk, dtype=F32)
    diff = idx[:, None] - idx[None, :]
    dmask = jnp.where(diff >= 0, jnp.exp(jnp.maximum(diff, 0.0)[None] * log_g[:, None, None]), 0.0)
    rep = lambda t: jnp.repeat(t, hd, axis=-1)
    qdec = rep(jnp.exp((idx + 1.0)[:, None] * log_g[None, :]))
    kdec = rep(jnp.exp((chunk - 1.0 - idx)[:, None] * log_g[None, :]))
    cdec = rep(jnp.exp(chunk * log_g)[None, :])

    full = lambda shape: pl.BlockSpec(shape, lambda b, s: (0,) * len(shape))
    return pl.pallas_call(
        _mix_kernel,
        name="mix",
        grid=(B, S // ts),
        in_specs=[
            pl.BlockSpec((1, ts, D), lambda b, s: (b, s, 0)),
            full((1, D)),
            full(w_in.shape),
            full((CONV_W, d_lru)),
            full((1, d_lru)),
            full(wg.shape),
            full(bg.shape),
            full((1, d_lru)),
            pl.BlockSpec((ts, hd), lambda b, s: (s, 0)),
            pl.BlockSpec((ts, hd), lambda b, s: (s, 0)),
            full(dmask.shape),
            full(qdec.shape),
            full(kdec.shape),
            full(cdec.shape),
            full(w_out.shape),
        ],
        out_specs=pl.BlockSpec((1, ts, D), lambda b, s: (b, s, 0)),
        out_shape=jax.ShapeDtypeStruct((B, S, D), F32),
        scratch_shapes=[
            pltpu.VMEM((ts + 2 * SUBLANES, d_lru), F32),
            pltpu.VMEM((SUBLANES, d_lru), F32),
            pltpu.VMEM((RET_HEADS, hd, hd), F32),
            pltpu.VMEM((ts, d_lru + d_ret), BF16),
        ],
        compiler_params=pltpu.CompilerParams(
            dimension_semantics=("arbitrary", "arbitrary"), vmem_limit_bytes=VMEM_LIMIT),
    )(x, g_mix.reshape(1, D), w_in.astype(BF16), conv_w, conv_b.reshape(1, d_lru), wg, bg,
      lam.reshape(1, d_lru), cos2, sin2, dmask, qdec, kdec, cdec, w_out.astype(BF16))


def _route_kernel(x_ref, g_ref, wr_ref, br_ref, hfp_ref, info_ref, cnt_ref, carry):
    tm, D = x_ref.shape

    @pl.when(pl.program_id(0) == 0)
    def _():
        carry[...] = jnp.zeros_like(carry)

    hf = _rms(x_ref[...], g_ref[...])
    hfp_ref[...] = _pack_bf16_pair(hf[:, :D // 2], hf[:, D // 2:])

    logits = jnp.dot(hf, wr_ref[...], preferred_element_type=F32, precision=lax.Precision.HIGHEST) + br_ref[...]
    lane = lax.broadcasted_iota(I32, logits.shape, 1)
    neg = jnp.float32(-jnp.inf)
    gmask = lane < N_GROUPS
    gl = jnp.where(gmask, logits, neg)
    gmax = jnp.max(gl, axis=-1, keepdims=True)
    g_sel = jnp.min(jnp.where(gl == gmax, lane, LANES), axis=-1, keepdims=True)
    g_w = 1.0 / jnp.sum(jnp.where(gmask, jnp.exp(logits - gmax), 0.0), axis=-1, keepdims=True)

    e_lane = lane - N_GROUPS
    emask = (e_lane >= 0) & (e_lane < N_EXPERTS) & ((e_lane >> 3) == g_sel)
    el = jnp.where(emask, logits, neg)
    v1 = jnp.max(el, axis=-1, keepdims=True)
    i1 = jnp.min(jnp.where(el == v1, e_lane, LANES), axis=-1, keepdims=True)
    el2 = jnp.where(e_lane == i1, neg, el)
    v2 = jnp.max(el2, axis=-1, keepdims=True)
    i2 = jnp.min(jnp.where(el2 == v2, e_lane, LANES), axis=-1, keepdims=True)
    t = jnp.exp(v2 - v1)
    w1 = g_w / (1.0 + t)
    w2 = g_w * (t / (1.0 + t))

    hit = (lane == i1) | (lane == i2)
    onehot = jnp.where(hit, 1.0, 0.0)
    rr = lax.broadcasted_iota(I32, (tm, tm), 0)
    cc = lax.broadcasted_iota(I32, (tm, tm), 1)
    tri = jnp.where(rr > cc, 1.0, 0.0).astype(BF16)
    prefix = jnp.dot(tri, onehot.astype(BF16), preferred_element_type=F32) + carry[0:1, :]
    r1 = jnp.sum(jnp.where(lane == i1, prefix, 0.0), axis=-1, keepdims=True)
    r2 = jnp.sum(jnp.where(lane == i2, prefix, 0.0), axis=-1, keepdims=True)
    carry[0:1, :] = carry[0:1, :] + jnp.sum(onehot, axis=0, keepdims=True)

    info = jnp.where(lane == 0, i1.astype(F32), 0.0)
    info = jnp.where(lane == 1, i2.astype(F32), info)
    info = jnp.where(lane == 2, w1, info)
    info = jnp.where(lane == 3, w2, info)
    info = jnp.where(lane == 4, r1, info)
    info = jnp.where(lane == 5, r2, info)
    info_ref[...] = info
    cnt_ref[...] = carry[...]


def _route(x1, g_ffn, w_grp, b_grp, w_exp, b_exp):
    T, D = x1.shape
    tm = ROUTE_TILE
    assert T % tm == 0 and N_GROUPS + N_EXPERTS <= LANES and T < (1 << 24)
    pad = LANES - N_GROUPS - N_EXPERTS
    wr = jnp.concatenate([w_grp, w_exp, jnp.zeros((D, pad), F32)], axis=1)
    br = jnp.concatenate([b_grp, b_exp, jnp.zeros((pad,), F32)]).reshape(1, LANES)
    return pl.pallas_call(
        _route_kernel,
        name="route",
        grid=(T // tm,),
        in_specs=[
            pl.BlockSpec((tm, D), lambda i: (i, 0)),
            pl.BlockSpec((1, D), lambda i: (0, 0)),
            pl.BlockSpec((D, LANES), lambda i: (0, 0)),
            pl.BlockSpec((1, LANES), lambda i: (0, 0)),
        ],
        out_specs=[
            pl.BlockSpec((tm, D // 2), lambda i: (i, 0)),
            pl.BlockSpec((tm, LANES), lambda i: (i, 0)),
            pl.BlockSpec((SUBLANES, LANES), lambda i: (0, 0)),
        ],
        out_shape=[
            jax.ShapeDtypeStruct((T, D // 2), U32),
            jax.ShapeDtypeStruct((T, LANES), F32),
            jax.ShapeDtypeStruct((SUBLANES, LANES), F32),
        ],
        scratch_shapes=[pltpu.VMEM((SUBLANES, LANES), F32)],
        compiler_params=pltpu.CompilerParams(
            dimension_semantics=("arbitrary",), vmem_limit_bytes=VMEM_LIMIT),
    )(x1, g_ffn.reshape(1, D), wr, br)


def _row_copy(src_ref, src_row, dst_ref, dst_row, sem):
    return pltpu.make_async_copy(src_ref.at[pl.ds(src_row, 1), :], dst_ref.at[pl.ds(dst_row, 1), :], sem)


def _dispatch_kernel(dest_ref, hfp_ref, xs_in_ref, xs_ref, sem):
    del xs_in_ref
    tm = hfp_ref.shape[0]

    def start(t, c):
        _row_copy(hfp_ref, t, xs_ref, dest_ref[2 * t], sem).start(priority=0)
        _row_copy(hfp_ref, t, xs_ref, dest_ref[2 * t + 1], sem).start(priority=1)
        return c

    lax.fori_loop(0, tm, start, 0, unroll=DMA_UNROLL)

    for _ in range(2 * tm):
        _row_copy(hfp_ref, 0, xs_ref, 0, sem).wait()


def _dispatch(dest, hfp, n_slots):
    T, W = hfp.shape
    tm = DISPATCH_TILE
    assert T % tm == 0
    xs0 = jnp.zeros((n_slots, W), U32)
    return pl.pallas_call(
        _dispatch_kernel,
        name="dispatch",
        grid=(T // tm,),
        in_specs=[
            pl.BlockSpec((2 * tm,), lambda i: (i,), memory_space=pltpu.SMEM),
            pl.BlockSpec((tm, W), lambda i: (i, 0)),
            pl.BlockSpec(memory_space=pl.ANY),
        ],
        out_specs=pl.BlockSpec(memory_space=pl.ANY),
        out_shape=jax.ShapeDtypeStruct((n_slots, W), U32),
        scratch_shapes=[pltpu.SemaphoreType.DMA],
        input_output_aliases={2: 0},
        compiler_params=pltpu.CompilerParams(
            dimension_semantics=("arbitrary",), has_side_effects=True, vmem_limit_bytes=VMEM_LIMIT),
    )(dest, hfp, xs0)


def _experts_kernel(bexp_ref, nvalid_ref, xs_ref, w1_ref, w3_ref, w2_ref, ys_ref, w1b, w3b, w2b):
    i = pl.program_id(0)
    half = xs_ref.shape[1]
    prev = bexp_ref[jnp.maximum(i - 1, 0)]
    valid = i < nvalid_ref[0]

    @pl.when(valid & ((i == 0) | (bexp_ref[i] != prev)))
    def _():
        w1b[...] = w1_ref[0].astype(BF16)
        w3b[...] = w3_ref[0].astype(BF16)
        w2b[...] = w2_ref[0].astype(BF16)

    @pl.when(valid)
    def _():
        hi, lo = _unpack_bf16_pair(xs_ref[...])
        hi = hi.astype(BF16)
        lo = lo.astype(BF16)
        a1 = (jnp.dot(hi, w1b[:half, :], preferred_element_type=F32)
              + jnp.dot(lo, w1b[half:, :], preferred_element_type=F32))
        a3 = (jnp.dot(hi, w3b[:half, :], preferred_element_type=F32)
              + jnp.dot(lo, w3b[half:, :], preferred_element_type=F32))
        hmid = ((a1 * _sigmoid(a1)) * a3).astype(BF16)
        y = jnp.dot(hmid, w2b[...], preferred_element_type=F32)
        ys_ref[...] = _pack_bf16_pair(y[:, :half], y[:, half:])


def _experts(block_exp, n_valid, xs, w1, w3, w2):
    n_slots, W = xs.shape
    bm = EXPERT_BLOCK
    n_blocks = n_slots // bm
    E, D, DE = w1.shape
    assert D == 2 * W
    clamp = lambda i, be, nv: jnp.minimum(i, nv[0] - 1)
    grid_spec = pltpu.PrefetchScalarGridSpec(
        num_scalar_prefetch=2,
        grid=(n_blocks,),
        in_specs=[
            pl.BlockSpec((bm, W), lambda i, be, nv: (clamp(i, be, nv), 0)),
            pl.BlockSpec((1, D, DE), lambda i, be, nv: (be[clamp(i, be, nv)], 0, 0)),
            pl.BlockSpec((1, D, DE), lambda i, be, nv: (be[clamp(i, be, nv)], 0, 0)),
            pl.BlockSpec((1, DE, D), lambda i, be, nv: (be[clamp(i, be, nv)], 0, 0)),
        ],
        out_specs=pl.BlockSpec((bm, W), lambda i, be, nv: (clamp(i, be, nv), 0)),
        scratch_shapes=[
            pltpu.VMEM((D, DE), BF16),
            pltpu.VMEM((D, DE), BF16),
            pltpu.VMEM((DE, D), BF16),
        ],
    )
    return pl.pallas_call(
        _experts_kernel,
        name="experts",
        grid_spec=grid_spec,
        out_shape=jax.ShapeDtypeStruct((n_slots, W), U32),
        input_output_aliases={2: 0},
        compiler_params=pltpu.CompilerParams(
            dimension_semantics=("arbitrary",), vmem_limit_bytes=VMEM_LIMIT),
    )(block_exp, n_valid, xs, w1, w3, w2)


def _combine_kernel(dest_ref, dest_next_ref, x1_ref, p_ref, info_ref, ys_ref, gple_ref, wgate_ref, bgate_ref,
                    wproj_ref, gfin_ref, o_ref, gbuf, sems, *, final_norm):
    tm = x1_ref.shape[0]
    i = pl.program_id(0)
    slot = lax.rem(i, 2)

    def gather(idx_ref, s):
        def start(t, c):
            _row_copy(ys_ref, idx_ref[2 * t], gbuf.at[s, 0], t, sems.at[s]).start(priority=0)
            _row_copy(ys_ref, idx_ref[2 * t + 1], gbuf.at[s, 1], t, sems.at[s]).start(priority=1)
            return c
        lax.fori_loop(0, tm, start, 0, unroll=DMA_UNROLL)

    @pl.when(i == 0)
    def _():
        gather(dest_ref, 0)

    @pl.when(i + 1 < pl.num_programs(0))
    def _():
        gather(dest_next_ref, 1 - slot)

    pp = jnp.dot(p_ref[...].astype(BF16), wproj_ref[...], preferred_element_type=F32)

    for _ in range(2 * tm):
        _row_copy(ys_ref, 0, gbuf.at[slot, 0], 0, sems.at[slot]).wait()

    info = info_ref[...]
    w_a = info[:, 2:3]
    w_b = info[:, 3:4]
    a_hi, a_lo = _unpack_bf16_pair(gbuf[slot, 0])
    b_hi, b_lo = _unpack_bf16_pair(gbuf[slot, 1])
    moe = jnp.concatenate([a_hi * w_a + b_hi * w_b, a_lo * w_a + b_lo * w_b], axis=1)
    x2 = x1_ref[...] + moe
    gate = _sigmoid(jnp.dot(_rms(x2, gple_ref[...]).astype(BF16), wgate_ref[...], preferred_element_type=F32)
                    + bgate_ref[...])
    x3 = x2 + gate * pp
    o_ref[...] = _rms(x3, gfin_ref[...]) if final_norm else x3


def _combine(dest, x1, p, info, ys, g_ple, w_gate, b_gate, w_proj, g_final, final_norm):
    T, D = x1.shape
    W = ys.shape[1]
    PD = p.shape[1]
    tm = COMBINE_TILE
    assert T % tm == 0
    n_steps = T // tm
    const = lambda shape: pl.BlockSpec(shape, lambda i: (0,) * len(shape))
    return pl.pallas_call(
        functools.partial(_combine_kernel, final_norm=final_norm),
        name="combine",
        grid=(n_steps,),
        in_specs=[
            pl.BlockSpec((2 * tm,), lambda i: (i,), memory_space=pltpu.SMEM),
            pl.BlockSpec((2 * tm,), lambda i: (jnp.minimum(i + 1, n_steps - 1),), memory_space=pltpu.SMEM),
            pl.BlockSpec((tm, D), lambda i: (i, 0)),
            pl.BlockSpec((tm, PD), lambda i: (i, 0)),
            pl.BlockSpec((tm, LANES), lambda i: (i, 0)),
            pl.BlockSpec(memory_space=pl.ANY),
            const((1, D)),
            const((D, D)),
            const((1, D)),
            const((PD, D)),
            const((1, D)),
        ],
        out_specs=pl.BlockSpec((tm, D), lambda i: (i, 0)),
        out_shape=jax.ShapeDtypeStruct((T, D), F32),
        scratch_shapes=[pltpu.VMEM((2, 2, tm, W), U32), pltpu.SemaphoreType.DMA((2,))],
        compiler_params=pltpu.CompilerParams(
            dimension_semantics=("arbitrary",), vmem_limit_bytes=VMEM_LIMIT),
    )(dest, dest, x1, p, info, ys, g_ple.reshape(1, D), w_gate.astype(BF16), b_gate.reshape(1, D),
      w_proj.astype(BF16), g_final.reshape(1, D))


def _slot_plan(info, counts_f, n_assign):
    bm = EXPERT_BLOCK
    counts = counts_f[0, :N_EXPERTS].astype(I32)
    padded = (counts + bm - 1) // bm * bm
    pad_end = jnp.cumsum(padded)
    pad_start = pad_end - padded
    n_blocks = -(-n_assign // bm) + N_EXPERTS
    block_start = jnp.arange(n_blocks, dtype=I32) * bm
    block_exp = jnp.minimum(jnp.sum((pad_end[None, :] <= block_start[:, None]).astype(I32), axis=1), N_EXPERTS - 1)
    n_valid = (pad_end[-1] // bm).astype(I32).reshape(1)
    e = info[:, 0:2].astype(I32)
    rank = info[:, 4:6].astype(I32)
    hit = e[:, :, None] == jnp.arange(N_EXPERTS, dtype=I32)[None, None, :]
    dest = (jnp.sum(jnp.where(hit, pad_start[None, None, :], 0), axis=-1) + rank).reshape(-1)
    return dest, block_exp, n_valid, n_blocks * bm


def kernel(x, p, g_mix, w_in, conv_w, conv_b, lru_wa, lru_ba, lru_wx, lru_bx, lru_lambda, w_out, g_ffn, w_router_group, b_router_group, w_router_expert, b_router_expert, w1, w3, w2, g_ple, w_ple_gate, b_ple_gate, w_ple_proj, g_final):
    B, S, D = x.shape
    T = B * S
    depth = g_mix.shape[0]
    for l in range(depth):
        x1 = _mix(x, g_mix[l], w_in[l], conv_w[l], conv_b[l], lru_wa[l], lru_ba[l], lru_wx[l], lru_bx[l],
                  lru_lambda[l], w_out[l]).reshape(T, D)
        hfp, info, counts = _route(x1, g_ffn[l], w_router_group[l], b_router_group[l],
                                   w_router_expert[l], b_router_expert[l])
        dest, block_exp, n_valid, n_slots = _slot_plan(info, counts, 2 * T)
        xs = _dispatch(dest, hfp, n_slots)
        ys = _experts(block_exp, n_valid, xs, w1[l], w3[l], w2[l])
        x = _combine(dest, x1, p[l].reshape(T, -1), info, ys, g_ple[l], w_ple_gate[l], b_ple_gate[l],
                     w_ple_proj[l], g_final, l == depth - 1).reshape(B, S, D)
    return x
```

```python
import functools

import jax
import jax.numpy as jnp
from jax import lax
from jax.experimental import pallas as pl
from jax.experimental.pallas import tpu as pltpu

F32 = jnp.float32
BF16 = jnp.bfloat16
I32 = jnp.int32

EPS = 1e-6
LRU_C = 8.0
CONV_W = 4
RET_HEADS = 4
ROPE_BASE = 10000.0
N_GROUPS = 8
EXPERTS_PER_GROUP = 8
N_EXPERTS = N_GROUPS * EXPERTS_PER_GROUP

LANES = 128
SUBLANES = 8
VMEM_LIMIT = 56 * 1024 * 1024

MIX_TILE = 512
RET_CHUNK = 256
ROUTE_TILE = 512
DISPATCH_TILE = 512
EXPERT_BLOCK = 256
COMBINE_TILE = 256
DMA_UNROLL = 8


def _rms(x, g):
    ms = jnp.mean(x * x, axis=-1, keepdims=True)
    return (x * lax.rsqrt(ms + EPS)) * g


def _sigmoid(z):
    return 1.0 / (1.0 + jnp.exp(-z))


def _store_token_tiles(tile_ref, x):
    n = x.shape[0]
    for c in range(SUBLANES):
        tile_ref[pl.ds(c, n, stride=SUBLANES), :] = x[:, c * LANES:(c + 1) * LANES]


def _load_token_chunk(tile_ref, c, n):
    return tile_ref[pl.ds(c, n, stride=SUBLANES), :]


def _tile_copy(src_ref, src_row, dst_ref, dst_row, sem):
    return pltpu.make_async_copy(src_ref.at[pl.ds(pl.multiple_of(src_row, SUBLANES), SUBLANES), :],
                                 dst_ref.at[pl.ds(pl.multiple_of(dst_row, SUBLANES), SUBLANES), :], sem)


def _shift_rows(x, d, fill, row):
    return jnp.where(row >= d, pltpu.roll(x, d, 0), fill)


def _linear_scan(a, u):
    n = a.shape[0]
    row = lax.broadcasted_iota(I32, a.shape, 0)
    d = 1
    while d < n:
        a_sh = _shift_rows(a, d, 1.0, row)
        u_sh = _shift_rows(u, d, 0.0, row)
        u = u + a * u_sh
        a = a * a_sh
        d *= 2
    return u


def _mix_kernel(x_ref, gmix_ref, win_ref, convw_ref, convb_ref, wg_ref, bg_ref, lam_ref,
                cos_ref, sin_ref, dmask_ref, qdec_ref, kdec_ref, cdec_ref, wout_ref,
                o_ref, xl_ext, hcar, state, ybuf):
    ts = x_ref.shape[1]
    d_lru = lam_ref.shape[1]
    d_ret = qdec_ref.shape[1]
    hd = d_ret // RET_HEADS
    chunk = dmask_ref.shape[1]
    half = d_lru // 2

    @pl.when(pl.program_id(1) == 0)
    def _():
        xl_ext[0:SUBLANES, :] = jnp.zeros((SUBLANES, d_lru), F32)
        hcar[...] = jnp.zeros_like(hcar)
        state[...] = jnp.zeros_like(state)

    x = x_ref[0]
    h = _rms(x, gmix_ref[...]).astype(BF16)

    def proj(lo, width):
        return jnp.dot(h, win_ref[:, lo:lo + width], preferred_element_type=F32)

    xl = proj(0, d_lru)
    xl_ext[SUBLANES:SUBLANES + ts, :] = xl
    xc = convb_ref[...] + xl_ext[SUBLANES - 3:SUBLANES - 3 + ts, :] * convw_ref[0:1, :]
    for k in range(1, CONV_W):
        off = SUBLANES - (CONV_W - 1) + k
        xc = xc + xl_ext[off:off + ts, :] * convw_ref[k:k + 1, :]
    xl_ext[0:SUBLANES, :] = xl_ext[ts:ts + SUBLANES, :]

    xcb = xc.astype(BF16)
    g0 = jnp.dot(xcb[:, :half], wg_ref[0], preferred_element_type=F32)
    g1 = jnp.dot(xcb[:, half:], wg_ref[1], preferred_element_type=F32)
    ra = jnp.concatenate([g0[:, :half], g1[:, :half]], axis=1) + bg_ref[:, :d_lru]
    ix = jnp.concatenate([g0[:, half:], g1[:, half:]], axis=1) + bg_ref[:, d_lru:]
    r = _sigmoid(ra)
    i_gate = _sigmoid(ix)
    z = -lam_ref[...]
    softplus = jnp.maximum(z, 0.0) + jnp.log1p(jnp.exp(-jnp.abs(z)))
    log_a = (-LRU_C * r) * softplus
    a = jnp.exp(log_a)
    u = jnp.sqrt(1.0 - a * a) * (i_gate * xc)
    row = lax.broadcasted_iota(I32, a.shape, 0)
    u = u + jnp.where(row == 0, a * hcar[0:1, :], 0.0)
    hseq = _linear_scan(a, u)
    hcar[0:1, :] = hseq[ts - 1:ts, :]
    gl = proj(d_lru, d_lru)
    ybuf[:, 0:d_lru] = (hseq * jax.nn.gelu(gl)).astype(BF16)

    base = 2 * d_lru
    q = proj(base, d_ret)
    k = proj(base + d_ret, d_ret)
    v = proj(base + 2 * d_ret, d_ret)
    gr = proj(base + 3 * d_ret, d_ret)
    scale = hd ** -0.5
    for hh in range(RET_HEADS):
        ls = slice(hh * hd, (hh + 1) * hd)
        for c in range(ts // chunk):
            rs = slice(c * chunk, (c + 1) * chunk)
            cosv = cos_ref[rs, :]
            sinv = sin_ref[rs, :]
            qh = q[rs, ls]
            kh = k[rs, ls]
            qr = qh * cosv + pltpu.roll(qh, hd // 2, 1) * sinv
            kr = (kh * cosv + pltpu.roll(kh, hd // 2, 1) * sinv) * scale
            vb = v[rs, ls].astype(BF16)
            scores = lax.dot_general(qr.astype(BF16), kr.astype(BF16), (((1,), (1,)), ((), ())),
                                     preferred_element_type=F32)
            inner = jnp.dot((scores * dmask_ref[hh]).astype(BF16), vb, preferred_element_type=F32)
            st = state[hh]
            cross = jnp.dot((qr * qdec_ref[:, ls]).astype(BF16), st.astype(BF16), preferred_element_type=F32)
            kv = lax.dot_general((kr * kdec_ref[:, ls]).astype(BF16), vb, (((0,), (0,)), ((), ())),
                                 preferred_element_type=F32)
            state[hh] = st * cdec_ref[:, ls] + kv
            o = inner + cross
            mu = jnp.mean(o, axis=-1, keepdims=True)
            oc = o - mu
            var = jnp.mean(oc * oc, axis=-1, keepdims=True)
            on = oc * lax.rsqrt(var + EPS)
            grh = gr[rs, ls]
            ybuf[rs, d_lru + hh * hd:d_lru + (hh + 1) * hd] = ((grh * _sigmoid(grh)) * on).astype(BF16)

    o_ref[0] = x + jnp.dot(ybuf[...], wout_ref[...], preferred_element_type=F32)


def _mix(x, g_mix, w_in, conv_w, conv_b, lru_wa, lru_ba, lru_wx, lru_bx, lam, w_out):
    B, S, D = x.shape
    d_lru = lam.shape[0]
    d_ret = (w_in.shape[1] - 2 * d_lru) // 4
    hd = d_ret // RET_HEADS
    ts, chunk = MIX_TILE, RET_CHUNK
    assert S % ts == 0 and ts % chunk == 0 and d_lru % (2 * LANES) == 0 and hd == LANES
    nb, bw = lru_wa.shape[0], lru_wa.shape[1]
    half = d_lru // 2
    per_half = nb // 2

    def blockdiag(w):
        out = jnp.zeros((half, half), F32)
        for j in range(per_half):
            out = out.at[j * bw:(j + 1) * bw, j * bw:(j + 1) * bw].set(w[j])
        return out
    wg = jnp.stack([jnp.concatenate([blockdiag(lru_wa[hf * per_half:(hf + 1) * per_half]),
                                     blockdiag(lru_wx[hf * per_half:(hf + 1) * per_half])], axis=1)
                    for hf in range(2)]).astype(BF16)
    bg = jnp.concatenate([lru_ba.reshape(1, d_lru), lru_bx.reshape(1, d_lru)], axis=1)

    inv = ROPE_BASE ** (-jnp.arange(hd // 2, dtype=F32) / (hd // 2))
    ang = jnp.arange(S, dtype=F32)[:, None] * inv[None, :]
    cos2 = jnp.concatenate([jnp.cos(ang), jnp.cos(ang)], axis=1)
    sin2 = jnp.concatenate([-jnp.sin(ang), jnp.sin(ang)], axis=1)
    log_g = jnp.log(1.0 - 2.0 ** (-5.0 - jnp.arange(RET_HEADS, dtype=F32)))
    idx = jnp.arange(chunk, dtype=F32)
    diff = idx[:, None] - idx[None, :]
    dmask = jnp.where(diff >= 0, jnp.exp(jnp.maximum(diff, 0.0)[None] * log_g[:, None, None]), 0.0)
    rep = lambda t: jnp.repeat(t, hd, axis=-1)
    qdec = rep(jnp.exp((idx + 1.0)[:, None] * log_g[None, :]))
    kdec = rep(jnp.exp((chunk - 1.0 - idx)[:, None] * log_g[None, :]))
    cdec = rep(jnp.exp(chunk * log_g)[None, :])

    full = lambda shape: pl.BlockSpec(shape, lambda b, s: (0,) * len(shape))
    return pl.pallas_call(
        _mix_kernel,
        name="mix",
        grid=(B, S // ts),
        in_specs=[
            pl.BlockSpec((1, ts, D), lambda b, s: (b, s, 0)),
            full((1, D)),
            full(w_in.shape),
            full((CONV_W, d_lru)),
            full((1, d_lru)),
            full(wg.shape),
            full(bg.shape),
            full((1, d_lru)),
            pl.BlockSpec((ts, hd), lambda b, s: (s, 0)),
            pl.BlockSpec((ts, hd), lambda b, s: (s, 0)),
            full(dmask.shape),
            full(qdec.shape),
            full(kdec.shape),
            full(cdec.shape),
            full(w_out.shape),
        ],
        out_specs=pl.BlockSpec((1, ts, D), lambda b, s: (b, s, 0)),
        out_shape=jax.ShapeDtypeStruct((B, S, D), F32),
        scratch_shapes=[
            pltpu.VMEM((ts + 2 * SUBLANES, d_lru), F32),
            pltpu.VMEM((SUBLANES, d_lru), F32),
            pltpu.VMEM((RET_HEADS, hd, hd), F32),
            pltpu.VMEM((ts, d_lru + d_ret), BF16),
        ],
        compiler_params=pltpu.CompilerParams(
            dimension_semantics=("arbitrary", "arbitrary"), vmem_limit_bytes=VMEM_LIMIT),
    )(x, g_mix.reshape(1, D), w_in.astype(BF16), conv_w, conv_b.reshape(1, d_lru), wg, bg,
      lam.reshape(1, d_lru), cos2, sin2, dmask, qdec, kdec, cdec, w_out.astype(BF16))


def _route_kernel(x_ref, g_ref, wr_ref, br_ref, hft_ref, info_ref, cnt_ref, carry):
    tm, D = x_ref.shape

    @pl.when(pl.program_id(0) == 0)
    def _():
        carry[...] = jnp.zeros_like(carry)

    hf = _rms(x_ref[...], g_ref[...])
    _store_token_tiles(hft_ref, hf)

    logits = jnp.dot(hf, wr_ref[...], preferred_element_type=F32, precision=lax.Precision.HIGHEST) + br_ref[...]
    lane = lax.broadcasted_iota(I32, logits.shape, 1)
    neg = jnp.float32(-jnp.inf)
    gmask = lane < N_GROUPS
    gl = jnp.where(gmask, logits, neg)
    gmax = jnp.max(gl, axis=-1, keepdims=True)
    g_sel = jnp.min(jnp.where(gl == gmax, lane, LANES), axis=-1, keepdims=True)
    g_w = 1.0 / jnp.sum(jnp.where(gmask, jnp.exp(logits - gmax), 0.0), axis=-1, keepdims=True)

    e_lane = lane - N_GROUPS
    emask = (e_lane >= 0) & (e_lane < N_EXPERTS) & ((e_lane >> 3) == g_sel)
    el = jnp.where(emask, logits, neg)
    v1 = jnp.max(el, axis=-1, keepdims=True)
    i1 = jnp.min(jnp.where(el == v1, e_lane, LANES), axis=-1, keepdims=True)
    el2 = jnp.where(e_lane == i1, neg, el)
    v2 = jnp.max(el2, axis=-1, keepdims=True)
    i2 = jnp.min(jnp.where(el2 == v2, e_lane, LANES), axis=-1, keepdims=True)
    t = jnp.exp(v2 - v1)
    w1 = g_w / (1.0 + t)
    w2 = g_w * (t / (1.0 + t))

    hit = (lane == i1) | (lane == i2)
    onehot = jnp.where(hit, 1.0, 0.0)
    rr = lax.broadcasted_iota(I32, (tm, tm), 0)
    cc = lax.broadcasted_iota(I32, (tm, tm), 1)
    tri = jnp.where(rr > cc, 1.0, 0.0).astype(BF16)
    prefix = jnp.dot(tri, onehot.astype(BF16), preferred_element_type=F32) + carry[0:1, :]
    r1 = jnp.sum(jnp.where(lane == i1, prefix, 0.0), axis=-1, keepdims=True)
    r2 = jnp.sum(jnp.where(lane == i2, prefix, 0.0), axis=-1, keepdims=True)
    carry[0:1, :] = carry[0:1, :] + jnp.sum(onehot, axis=0, keepdims=True)

    info = jnp.where(lane == 0, i1.astype(F32), 0.0)
    info = jnp.where(lane == 1, i2.astype(F32), info)
    info = jnp.where(lane == 2, w1, info)
    info = jnp.where(lane == 3, w2, info)
    info = jnp.where(lane == 4, r1, info)
    info = jnp.where(lane == 5, r2, info)
    info_ref[...] = info
    cnt_ref[...] = carry[...]


def _route(x1, g_ffn, w_grp, b_grp, w_exp, b_exp):
    T, D = x1.shape
    tm = ROUTE_TILE
    assert T % tm == 0 and N_GROUPS + N_EXPERTS <= LANES and T < (1 << 24) and D == SUBLANES * LANES
    pad = LANES - N_GROUPS - N_EXPERTS
    wr = jnp.concatenate([w_grp, w_exp, jnp.zeros((D, pad), F32)], axis=1)
    br = jnp.concatenate([b_grp, b_exp, jnp.zeros((pad,), F32)]).reshape(1, LANES)
    return pl.pallas_call(
        _route_kernel,
        name="route",
        grid=(T // tm,),
        in_specs=[
            pl.BlockSpec((tm, D), lambda i: (i, 0)),
            pl.BlockSpec((1, D), lambda i: (0, 0)),
            pl.BlockSpec((D, LANES), lambda i: (0, 0)),
            pl.BlockSpec((1, LANES), lambda i: (0, 0)),
        ],
        out_specs=[
            pl.BlockSpec((tm * SUBLANES, LANES), lambda i: (i, 0)),
            pl.BlockSpec((tm, LANES), lambda i: (i, 0)),
            pl.BlockSpec((SUBLANES, LANES), lambda i: (0, 0)),
        ],
        out_shape=[
            jax.ShapeDtypeStruct((T * SUBLANES, LANES), F32),
            jax.ShapeDtypeStruct((T, LANES), F32),
            jax.ShapeDtypeStruct((SUBLANES, LANES), F32),
        ],
        scratch_shapes=[pltpu.VMEM((SUBLANES, LANES), F32)],
        compiler_params=pltpu.CompilerParams(
            dimension_semantics=("arbitrary",), vmem_limit_bytes=VMEM_LIMIT),
    )(x1, g_ffn.reshape(1, D), wr, br)


def _dispatch_kernel(dest_ref, hft_ref, xs_in_ref, xs_ref, sem):
    del xs_in_ref
    tm = hft_ref.shape[0] // SUBLANES

    def start(t, c):
        _tile_copy(hft_ref, t * SUBLANES, xs_ref, dest_ref[2 * t], sem).start(priority=0)
        _tile_copy(hft_ref, t * SUBLANES, xs_ref, dest_ref[2 * t + 1], sem).start(priority=1)
        return c

    lax.fori_loop(0, tm, start, 0, unroll=DMA_UNROLL)

    for _ in range(2 * tm):
        _tile_copy(hft_ref, 0, xs_ref, 0, sem).wait()


def _dispatch(dest, hft, n_slots):
    T = hft.shape[0] // SUBLANES
    tm = DISPATCH_TILE
    assert T % tm == 0
    xs0 = jnp.zeros((n_slots * SUBLANES, LANES), F32)
    return pl.pallas_call(
        _dispatch_kernel,
        name="dispatch",
        grid=(T // tm,),
        in_specs=[
            pl.BlockSpec((2 * tm,), lambda i: (i,), memory_space=pltpu.SMEM),
            pl.BlockSpec((tm * SUBLANES, LANES), lambda i: (i, 0)),
            pl.BlockSpec(memory_space=pl.ANY),
        ],
        out_specs=pl.BlockSpec(memory_space=pl.ANY),
        out_shape=jax.ShapeDtypeStruct(xs0.shape, F32),
        scratch_shapes=[pltpu.SemaphoreType.DMA],
        input_output_aliases={2: 0},
        compiler_params=pltpu.CompilerParams(
            dimension_semantics=("arbitrary",), has_side_effects=True, vmem_limit_bytes=VMEM_LIMIT),
    )(dest, hft, xs0)


def _experts_kernel(bexp_ref, nvalid_ref, xs_ref, w1_ref, w3_ref, w2_ref, ys_ref, w1b, w3b, w2b, xb):
    i = pl.program_id(0)
    bm = xb.shape[0]
    prev = bexp_ref[jnp.maximum(i - 1, 0)]
    valid = i < nvalid_ref[0]

    @pl.when(valid & ((i == 0) | (bexp_ref[i] != prev)))
    def _():
        w1b[...] = w1_ref[0].astype(BF16)
        w3b[...] = w3_ref[0].astype(BF16)
        w2b[...] = w2_ref[0].astype(BF16)

    @pl.when(valid)
    def _():
        for c in range(SUBLANES):
            xb[:, c * LANES:(c + 1) * LANES] = _load_token_chunk(xs_ref, c, bm).astype(BF16)
        x = xb[...]
        a1 = jnp.dot(x, w1b[...], preferred_element_type=F32)
        a3 = jnp.dot(x, w3b[...], preferred_element_type=F32)
        hmid = ((a1 * _sigmoid(a1)) * a3).astype(BF16)
        _store_token_tiles(ys_ref, jnp.dot(hmid, w2b[...], preferred_element_type=F32))


def _experts(block_exp, n_valid, xs, w1, w3, w2):
    bm = EXPERT_BLOCK
    n_blocks = xs.shape[0] // (bm * SUBLANES)
    E, D, DE = w1.shape
    assert D == SUBLANES * LANES
    clamp = lambda i, be, nv: jnp.minimum(i, nv[0] - 1)
    grid_spec = pltpu.PrefetchScalarGridSpec(
        num_scalar_prefetch=2,
        grid=(n_blocks,),
        in_specs=[
            pl.BlockSpec((bm * SUBLANES, LANES), lambda i, be, nv: (clamp(i, be, nv), 0)),
            pl.BlockSpec((1, D, DE), lambda i, be, nv: (be[clamp(i, be, nv)], 0, 0)),
            pl.BlockSpec((1, D, DE), lambda i, be, nv: (be[clamp(i, be, nv)], 0, 0)),
            pl.BlockSpec((1, DE, D), lambda i, be, nv: (be[clamp(i, be, nv)], 0, 0)),
        ],
        out_specs=pl.BlockSpec((bm * SUBLANES, LANES), lambda i, be, nv: (clamp(i, be, nv), 0)),
        scratch_shapes=[
            pltpu.VMEM((D, DE), BF16),
            pltpu.VMEM((D, DE), BF16),
            pltpu.VMEM((DE, D), BF16),
            pltpu.VMEM((bm, D), BF16),
        ],
    )
    return pl.pallas_call(
        _experts_kernel,
        name="experts",
        grid_spec=grid_spec,
        out_shape=jax.ShapeDtypeStruct(xs.shape, F32),
        input_output_aliases={2: 0},
        compiler_params=pltpu.CompilerParams(
            dimension_semantics=("arbitrary",), vmem_limit_bytes=VMEM_LIMIT),
    )(block_exp, n_valid, xs, w1, w3, w2)


def _combine_kernel(dest_ref, dest_next_ref, x1_ref, p_ref, info_ref, ys_ref, gple_ref, wgate_ref, bgate_ref,
                    wproj_ref, gfin_ref, o_ref, gbuf, sems, *, final_norm):
    tm = x1_ref.shape[0]
    i = pl.program_id(0)
    slot = lax.rem(i, 2)

    def gather(idx_ref, s):
        def start(t, c):
            _tile_copy(ys_ref, idx_ref[2 * t], gbuf.at[s, 0], t * SUBLANES, sems.at[s]).start(priority=0)
            _tile_copy(ys_ref, idx_ref[2 * t + 1], gbuf.at[s, 1], t * SUBLANES, sems.at[s]).start(priority=1)
            return c
        lax.fori_loop(0, tm, start, 0, unroll=DMA_UNROLL)

    @pl.when(i == 0)
    def _():
        gather(dest_ref, 0)

    @pl.when(i + 1 < pl.num_programs(0))
    def _():
        gather(dest_next_ref, 1 - slot)

    pp = jnp.dot(p_ref[...].astype(BF16), wproj_ref[...], preferred_element_type=F32)

    for _ in range(2 * tm):
        _tile_copy(ys_ref, 0, gbuf.at[slot, 0], 0, sems.at[slot]).wait()

    info = info_ref[...]
    w_a = info[:, 2:3]
    w_b = info[:, 3:4]
    moe = jnp.concatenate(
        [_load_token_chunk(gbuf.at[slot, 0], c, tm) * w_a + _load_token_chunk(gbuf.at[slot, 1], c, tm) * w_b
         for c in range(SUBLANES)], axis=1)
    x2 = x1_ref[...] + moe
    gate = _sigmoid(jnp.dot(_rms(x2, gple_ref[...]).astype(BF16), wgate_ref[...], preferred_element_type=F32)
                    + bgate_ref[...])
    x3 = x2 + gate * pp
    o_ref[...] = _rms(x3, gfin_ref[...]) if final_norm else x3


def _combine(dest, x1, p, info, ys, g_ple, w_gate, b_gate, w_proj, g_final, final_norm):
    T, D = x1.shape
    PD = p.shape[1]
    tm = COMBINE_TILE
    assert T % tm == 0 and D == SUBLANES * LANES
    n_steps = T // tm
    const = lambda shape: pl.BlockSpec(shape, lambda i: (0,) * len(shape))
    return pl.pallas_call(
        functools.partial(_combine_kernel, final_norm=final_norm),
        name="combine",
        grid=(n_steps,),
        in_specs=[
            pl.BlockSpec((2 * tm,), lambda i: (i,), memory_space=pltpu.SMEM),
            pl.BlockSpec((2 * tm,), lambda i: (jnp.minimum(i + 1, n_steps - 1),), memory_space=pltpu.SMEM),
            pl.BlockSpec((tm, D), lambda i: (i, 0)),
            pl.BlockSpec((tm, PD), lambda i: (i, 0)),
            pl.BlockSpec((tm, LANES), lambda i: (i, 0)),
            pl.BlockSpec(memory_space=pl.ANY),
            const((1, D)),
            const((D, D)),
            const((1, D)),
            const((PD, D)),
            const((1, D)),
        ],
        out_specs=pl.BlockSpec((tm, D), lambda i: (i, 0)),
        out_shape=jax.ShapeDtypeStruct((T, D), F32),
        scratch_shapes=[pltpu.VMEM((2, 2, tm * SUBLANES, LANES), F32), pltpu.SemaphoreType.DMA((2,))],
        compiler_params=pltpu.CompilerParams(
            dimension_semantics=("arbitrary",), vmem_limit_bytes=VMEM_LIMIT),
    )(dest, dest, x1, p, info, ys, g_ple.reshape(1, D), w_gate.astype(BF16), b_gate.reshape(1, D),
      w_proj.astype(BF16), g_final.reshape(1, D))


def _slot_plan(info, counts_f, n_assign):
    bm = EXPERT_BLOCK
    counts = counts_f[0, :N_EXPERTS].astype(I32)
    padded = (counts + bm - 1) // bm * bm
    pad_end = jnp.cumsum(padded)
    pad_start = pad_end - padded
    n_blocks = -(-n_assign // bm) + N_EXPERTS
    block_start = jnp.arange(n_blocks, dtype=I32) * bm
    block_exp = jnp.minimum(jnp.sum((pad_end[None, :] <= block_start[:, None]).astype(I32), axis=1), N_EXPERTS - 1)
    n_valid = (pad_end[-1] // bm).astype(I32).reshape(1)
    e = info[:, 0:2].astype(I32)
    rank = info[:, 4:6].astype(I32)
    hit = e[:, :, None] == jnp.arange(N_EXPERTS, dtype=I32)[None, None, :]
    dest = (jnp.sum(jnp.where(hit, pad_start[None, None, :], 0), axis=-1) + rank).reshape(-1)
    return dest * SUBLANES, block_exp, n_valid, n_blocks * bm


def kernel(x, p, g_mix, w_in, conv_w, conv_b, lru_wa, lru_ba, lru_wx, lru_bx, lru_lambda, w_out, g_ffn, w_router_group, b_router_group, w_router_expert, b_router_expert, w1, w3, w2, g_ple, w_ple_gate, b_ple_gate, w_ple_proj, g_final):
    B, S, D = x.shape
    T = B * S
    depth = g_mix.shape[0]
    for l in range(depth):
        x1 = _mix(x, g_mix[l], w_in[l], conv_w[l], conv_b[l], lru_wa[l], lru_ba[l], lru_wx[l], lru_bx[l],
                  lru_lambda[l], w_out[l]).reshape(T, D)
        hft, info, counts = _route(x1, g_ffn[l], w_router_group[l], b_router_group[l],
                                   w_router_expert[l], b_router_expert[l])
        dest, block_exp, n_valid, n_slots = _slot_plan(info, counts, 2 * T)
        xs = _dispatch(dest, hft, n_slots)
        ys = _experts(block_exp, n_valid, xs, w1[l], w3[l], w2[l])
        x = _combine(dest, x1, p[l].reshape(T, -1), info, ys, g_ple[l], w_ple_gate[l], b_ple_gate[l],
                     w_ple_proj[l], g_final, l == depth - 1).reshape(B, S, D)
    return x
```

```python
import functools

import jax
import jax.numpy as jnp
from jax import lax
from jax.experimental import pallas as pl
from jax.experimental.pallas import tpu as pltpu

F32 = jnp.float32
BF16 = jnp.bfloat16
I32 = jnp.int32
U32 = jnp.uint32

EPS = 1e-6
LRU_C = 8.0
CONV_W = 4
RET_HEADS = 4
ROPE_BASE = 10000.0
N_GROUPS = 8
EXPERTS_PER_GROUP = 8
N_EXPERTS = N_GROUPS * EXPERTS_PER_GROUP

LANES = 128
SUBLANES = 8
VMEM_LIMIT = 56 * 1024 * 1024

MIX_TILE = 512
RET_CHUNK = 256
ROUTE_TILE = 512
DISPATCH_TILE = 512
EXPERT_BLOCK = 256
COMBINE_TILE = 256
DMA_UNROLL = 8


def _rms(x, g):
    ms = jnp.mean(x * x, axis=-1, keepdims=True)
    return (x * lax.rsqrt(ms + EPS)) * g


def _sigmoid(z):
    return 1.0 / (1.0 + jnp.exp(-z))


TOKEN_ROWS = 4


def _pack_pair(a, b):
    ab = lax.bitcast_convert_type(a.astype(BF16).astype(F32), U32)
    bb = lax.bitcast_convert_type(b.astype(BF16).astype(F32), U32)
    return ab | (bb >> 16)


def _unpack_pair(w):
    hi = lax.bitcast_convert_type(w & jnp.uint32(0xFFFF0000), F32)
    lo = lax.bitcast_convert_type(w << 16, F32)
    return hi, lo


def _store_token_tiles(tile_ref, x):
    n = x.shape[0]
    half = TOKEN_ROWS * LANES
    for j in range(TOKEN_ROWS):
        tile_ref[pl.ds(j, n, stride=TOKEN_ROWS), :] = _pack_pair(x[:, j * LANES:(j + 1) * LANES],
                                                                 x[:, half + j * LANES:half + (j + 1) * LANES])


def _load_token_pair(tile_ref, j, n):
    return _unpack_pair(tile_ref[pl.ds(j, n, stride=TOKEN_ROWS), :])


def _tile_copy(src_ref, src_row, dst_ref, dst_row, sem):
    return pltpu.make_async_copy(src_ref.at[pl.ds(pl.multiple_of(src_row, TOKEN_ROWS), TOKEN_ROWS), :],
                                 dst_ref.at[pl.ds(pl.multiple_of(dst_row, TOKEN_ROWS), TOKEN_ROWS), :], sem)


def _shift_rows(x, d, fill, row):
    return jnp.where(row >= d, pltpu.roll(x, d, 0), fill)


def _linear_scan(a, u):
    n = a.shape[0]
    row = lax.broadcasted_iota(I32, a.shape, 0)
    d = 1
    while d < n:
        a_sh = _shift_rows(a, d, 1.0, row)
        u_sh = _shift_rows(u, d, 0.0, row)
        u = u + a * u_sh
        a = a * a_sh
        d *= 2
    return u


def _mix_kernel(x_ref, gmix_ref, win_ref, convw_ref, convb_ref, wg_ref, bg_ref, lam_ref,
                cos_ref, sin_ref, dmask_ref, qdec_ref, kdec_ref, cdec_ref, wout_ref,
                o_ref, xl_ext, hcar, state, ybuf):
    ts = x_ref.shape[1]
    d_lru = lam_ref.shape[1]
    d_ret = qdec_ref.shape[1]
    hd = d_ret // RET_HEADS
    chunk = dmask_ref.shape[1]
    half = d_lru // 2

    @pl.when(pl.program_id(1) == 0)
    def _():
        xl_ext[0:SUBLANES, :] = jnp.zeros((SUBLANES, d_lru), F32)
        hcar[...] = jnp.zeros_like(hcar)
        state[...] = jnp.zeros_like(state)

    x = x_ref[0]
    h = _rms(x, gmix_ref[...]).astype(BF16)

    def proj(lo, width):
        return jnp.dot(h, win_ref[:, lo:lo + width], preferred_element_type=F32)

    xl = proj(0, d_lru)
    xl_ext[SUBLANES:SUBLANES + ts, :] = xl
    xc = convb_ref[...] + xl_ext[SUBLANES - 3:SUBLANES - 3 + ts, :] * convw_ref[0:1, :]
    for k in range(1, CONV_W):
        off = SUBLANES - (CONV_W - 1) + k
        xc = xc + xl_ext[off:off + ts, :] * convw_ref[k:k + 1, :]
    xl_ext[0:SUBLANES, :] = xl_ext[ts:ts + SUBLANES, :]

    xcb = xc.astype(BF16)
    g0 = jnp.dot(xcb[:, :half], wg_ref[0], preferred_element_type=F32)
    g1 = jnp.dot(xcb[:, half:], wg_ref[1], preferred_element_type=F32)
    ra = jnp.concatenate([g0[:, :half], g1[:, :half]], axis=1) + bg_ref[:, :d_lru]
    ix = jnp.concatenate([g0[:, half:], g1[:, half:]], axis=1) + bg_ref[:, d_lru:]
    r = _sigmoid(ra)
    i_gate = _sigmoid(ix)
    z = -lam_ref[...]
    softplus = jnp.maximum(z, 0.0) + jnp.log1p(jnp.exp(-jnp.abs(z)))
    log_a = (-LRU_C * r) * softplus
    a = jnp.exp(log_a)
    u = jnp.sqrt(1.0 - a * a) * (i_gate * xc)
    row = lax.broadcasted_iota(I32, a.shape, 0)
    u = u + jnp.where(row == 0, a * hcar[0:1, :], 0.0)
    hseq = _linear_scan(a, u)
    hcar[0:1, :] = hseq[ts - 1:ts, :]
    gl = proj(d_lru, d_lru)
    ybuf[:, 0:d_lru] = (hseq * jax.nn.gelu(gl)).astype(BF16)

    base = 2 * d_lru
    q = proj(base, d_ret)
    k = proj(base + d_ret, d_ret)
    v = proj(base + 2 * d_ret, d_ret)
    gr = proj(base + 3 * d_ret, d_ret)
    scale = hd ** -0.5
    for hh in range(RET_HEADS):
        ls = slice(hh * hd, (hh + 1) * hd)
        for c in range(ts // chunk):
            rs = slice(c * chunk, (c + 1) * chunk)
            cosv = cos_ref[rs, :]
            sinv = sin_ref[rs, :]
            qh = q[rs, ls]
            kh = k[rs, ls]
            qr = qh * cosv + pltpu.roll(qh, hd // 2, 1) * sinv
            kr = (kh * cosv + pltpu.roll(kh, hd // 2, 1) * sinv) * scale
            vb = v[rs, ls].astype(BF16)
            scores = lax.dot_general(qr.astype(BF16), kr.astype(BF16), (((1,), (1,)), ((), ())),
                                     preferred_element_type=F32)
            inner = jnp.dot((scores * dmask_ref[hh]).astype(BF16), vb, preferred_element_type=F32)
            st = state[hh]
            cross = jnp.dot((qr * qdec_ref[:, ls]).astype(BF16), st.astype(BF16), preferred_element_type=F32)
            kv = lax.dot_general((kr * kdec_ref[:, ls]).astype(BF16), vb, (((0,), (0,)), ((), ())),
                                 preferred_element_type=F32)
            state[hh] = st * cdec_ref[:, ls] + kv
            o = inner + cross
            mu = jnp.mean(o, axis=-1, keepdims=True)
            oc = o - mu
            var = jnp.mean(oc * oc, axis=-1, keepdims=True)
            on = oc * lax.rsqrt(var + EPS)
            grh = gr[rs, ls]
            ybuf[rs, d_lru + hh * hd:d_lru + (hh + 1) * hd] = ((grh * _sigmoid(grh)) * on).astype(BF16)

    o_ref[0] = x + jnp.dot(ybuf[...], wout_ref[...], preferred_element_type=F32)


def _mix(x, g_mix, w_in, conv_w, conv_b, lru_wa, lru_ba, lru_wx, lru_bx, lam, w_out):
    B, S, D = x.shape
    d_lru = lam.shape[0]
    d_ret = (w_in.shape[1] - 2 * d_lru) // 4
    hd = d_ret // RET_HEADS
    ts, chunk = MIX_TILE, RET_CHUNK
    assert S % ts == 0 and ts % chunk == 0 and d_lru % (2 * LANES) == 0 and hd == LANES
    nb, bw = lru_wa.shape[0], lru_wa.shape[1]
    half = d_lru // 2
    per_half = nb // 2

    def blockdiag(w):
        out = jnp.zeros((half, half), F32)
        for j in range(per_half):
            out = out.at[j * bw:(j + 1) * bw, j * bw:(j + 1) * bw].set(w[j])
        return out
    wg = jnp.stack([jnp.concatenate([blockdiag(lru_wa[hf * per_half:(hf + 1) * per_half]),
                                     blockdiag(lru_wx[hf * per_half:(hf + 1) * per_half])], axis=1)
                    for hf in range(2)]).astype(BF16)
    bg = jnp.concatenate([lru_ba.reshape(1, d_lru), lru_bx.reshape(1, d_lru)], axis=1)

    inv = ROPE_BASE ** (-jnp.arange(hd // 2, dtype=F32) / (hd // 2))
    ang = jnp.arange(S, dtype=F32)[:, None] * inv[None, :]
    cos2 = jnp.concatenate([jnp.cos(ang), jnp.cos(ang)], axis=1)
    sin2 = jnp.concatenate([-jnp.sin(ang), jnp.sin(ang)], axis=1)
    log_g = jnp.log(1.0 - 2.0 ** (-5.0 - jnp.arange(RET_HEADS, dtype=F32)))
    idx = jnp.arange(chunk, dtype=F32)
    diff = idx[:, None] - idx[None, :]
    dmask = jnp.where(diff >= 0, jnp.exp(jnp.maximum(diff, 0.0)[None] * log_g[:, None, None]), 0.0)
    rep = lambda t: jnp.repeat(t, hd, axis=-1)
    qdec = rep(jnp.exp((idx + 1.0)[:, None] * log_g[None, :]))
    kdec = rep(jnp.exp((chunk - 1.0 - idx)[:, None] * log_g[None, :]))
    cdec = rep(jnp.exp(chunk * log_g)[None, :])

    full = lambda shape: pl.BlockSpec(shape, lambda b, s: (0,) * len(shape))
    return pl.pallas_call(
        _mix_kernel,
        name="mix",
        grid=(B, S // ts),
        in_specs=[
            pl.BlockSpec((1, ts, D), lambda b, s: (b, s, 0)),
            full((1, D)),
            full(w_in.shape),
            full((CONV_W, d_lru)),
            full((1, d_lru)),
            full(wg.shape),
            full(bg.shape),
            full((1, d_lru)),
            pl.BlockSpec((ts, hd), lambda b, s: (s, 0)),
            pl.BlockSpec((ts, hd), lambda b, s: (s, 0)),
            full(dmask.shape),
            full(qdec.shape),
            full(kdec.shape),
            full(cdec.shape),
            full(w_out.shape),
        ],
        out_specs=pl.BlockSpec((1, ts, D), lambda b, s: (b, s, 0)),
        out_shape=jax.ShapeDtypeStruct((B, S, D), F32),
        scratch_shapes=[
            pltpu.VMEM((ts + 2 * SUBLANES, d_lru), F32),
            pltpu.VMEM((SUBLANES, d_lru), F32),
            pltpu.VMEM((RET_HEADS, hd, hd), F32),
            pltpu.VMEM((ts, d_lru + d_ret), BF16),
        ],
        compiler_params=pltpu.CompilerParams(
            dimension_semantics=("arbitrary", "arbitrary"), vmem_limit_bytes=VMEM_LIMIT),
    )(x, g_mix.reshape(1, D), w_in.astype(BF16), conv_w, conv_b.reshape(1, d_lru), wg, bg,
      lam.reshape(1, d_lru), cos2, sin2, dmask, qdec, kdec, cdec, w_out.astype(BF16))


def _route_kernel(x_ref, g_ref, wr_ref, br_ref, hft_ref, info_ref, cnt_ref, carry):
    tm, D = x_ref.shape

    @pl.when(pl.program_id(0) == 0)
    def _():
        carry[...] = jnp.zeros_like(carry)

    hf = _rms(x_ref[...], g_ref[...])
    _store_token_tiles(hft_ref, hf)

    logits = jnp.dot(hf, wr_ref[...], preferred_element_type=F32, precision=lax.Precision.HIGHEST) + br_ref[...]
    lane = lax.broadcasted_iota(I32, logits.shape, 1)
    neg = jnp.float32(-jnp.inf)
    gmask = lane < N_GROUPS
    gl = jnp.where(gmask, logits, neg)
    gmax = jnp.max(gl, axis=-1, keepdims=True)
    g_sel = jnp.min(jnp.where(gl == gmax, lane, LANES), axis=-1, keepdims=True)
    g_w = 1.0 / jnp.sum(jnp.where(gmask, jnp.exp(logits - gmax), 0.0), axis=-1, keepdims=True)

    e_lane = lane - N_GROUPS
    emask = (e_lane >= 0) & (e_lane < N_EXPERTS) & ((e_lane >> 3) == g_sel)
    el = jnp.where(emask, logits, neg)
    v1 = jnp.max(el, axis=-1, keepdims=True)
    i1 = jnp.min(jnp.where(el == v1, e_lane, LANES), axis=-1, keepdims=True)
    el2 = jnp.where(e_lane == i1, neg, el)
    v2 = jnp.max(el2, axis=-1, keepdims=True)
    i2 = jnp.min(jnp.where(el2 == v2, e_lane, LANES), axis=-1, keepdims=True)
    t = jnp.exp(v2 - v1)
    w1 = g_w / (1.0 + t)
    w2 = g_w * (t / (1.0 + t))

    hit = (lane == i1) | (lane == i2)
    onehot = jnp.where(hit, 1.0, 0.0)
    rr = lax.broadcasted_iota(I32, (tm, tm), 0)
    cc = lax.broadcasted_iota(I32, (tm, tm), 1)
    tri = jnp.where(rr > cc, 1.0, 0.0).astype(BF16)
    prefix = jnp.dot(tri, onehot.astype(BF16), preferred_element_type=F32) + carry[0:1, :]
    r1 = jnp.sum(jnp.where(lane == i1, prefix, 0.0), axis=-1, keepdims=True)
    r2 = jnp.sum(jnp.where(lane == i2, prefix, 0.0), axis=-1, keepdims=True)
    carry[0:1, :] = carry[0:1, :] + jnp.sum(onehot, axis=0, keepdims=True)

    info = jnp.where(lane == 0, i1.astype(F32), 0.0)
    info = jnp.where(lane == 1, i2.astype(F32), info)
    info = jnp.where(lane == 2, w1, info)
    info = jnp.where(lane == 3, w2, info)
    info = jnp.where(lane == 4, r1, info)
    info = jnp.where(lane == 5, r2, info)
    info_ref[...] = info
    cnt_ref[...] = carry[...]


def _route(x1, g_ffn, w_grp, b_grp, w_exp, b_exp):
    T, D = x1.shape
    tm = ROUTE_TILE
    assert T % tm == 0 and N_GROUPS + N_EXPERTS <= LANES and T < (1 << 24) and D == 2 * TOKEN_ROWS * LANES
    pad = LANES - N_GROUPS - N_EXPERTS
    wr = jnp.concatenate([w_grp, w_exp, jnp.zeros((D, pad), F32)], axis=1)
    br = jnp.concatenate([b_grp, b_exp, jnp.zeros((pad,), F32)]).reshape(1, LANES)
    return pl.pallas_call(
        _route_kernel,
        name="route",
        grid=(T // tm,),
        in_specs=[
            pl.BlockSpec((tm, D), lambda i: (i, 0)),
            pl.BlockSpec((1, D), lambda i: (0, 0)),
            pl.BlockSpec((D, LANES), lambda i: (0, 0)),
            pl.BlockSpec((1, LANES), lambda i: (0, 0)),
        ],
        out_specs=[
            pl.BlockSpec((tm * TOKEN_ROWS, LANES), lambda i: (i, 0)),
            pl.BlockSpec((tm, LANES), lambda i: (i, 0)),
            pl.BlockSpec((SUBLANES, LANES), lambda i: (0, 0)),
        ],
        out_shape=[
            jax.ShapeDtypeStruct((T * TOKEN_ROWS, LANES), U32),
            jax.ShapeDtypeStruct((T, LANES), F32),
            jax.ShapeDtypeStruct((SUBLANES, LANES), F32),
        ],
        scratch_shapes=[pltpu.VMEM((SUBLANES, LANES), F32)],
        compiler_params=pltpu.CompilerParams(
            dimension_semantics=("arbitrary",), vmem_limit_bytes=VMEM_LIMIT),
    )(x1, g_ffn.reshape(1, D), wr, br)


def _dispatch_kernel(dest_ref, hft_ref, xs_in_ref, xs_ref, sem):
    del xs_in_ref
    tm = hft_ref.shape[0] // TOKEN_ROWS

    def start(t, c):
        _tile_copy(hft_ref, t * TOKEN_ROWS, xs_ref, dest_ref[2 * t], sem).start(priority=0)
        _tile_copy(hft_ref, t * TOKEN_ROWS, xs_ref, dest_ref[2 * t + 1], sem).start(priority=1)
        return c

    lax.fori_loop(0, tm, start, 0, unroll=DMA_UNROLL)

    for _ in range(2 * tm):
        _tile_copy(hft_ref, 0, xs_ref, 0, sem).wait()


def _dispatch(dest, hft, n_slots):
    T = hft.shape[0] // TOKEN_ROWS
    tm = DISPATCH_TILE
    assert T % tm == 0
    xs0 = jnp.zeros((n_slots * TOKEN_ROWS, LANES), U32)
    return pl.pallas_call(
        _dispatch_kernel,
        name="dispatch",
        grid=(T // tm,),
        in_specs=[
            pl.BlockSpec((2 * tm,), lambda i: (i,), memory_space=pltpu.SMEM),
            pl.BlockSpec((tm * TOKEN_ROWS, LANES), lambda i: (i, 0)),
            pl.BlockSpec(memory_space=pl.ANY),
        ],
        out_specs=pl.BlockSpec(memory_space=pl.ANY),
        out_shape=jax.ShapeDtypeStruct(xs0.shape, U32),
        scratch_shapes=[pltpu.SemaphoreType.DMA],
        input_output_aliases={2: 0},
        compiler_params=pltpu.CompilerParams(
            dimension_semantics=("arbitrary",), has_side_effects=True, vmem_limit_bytes=VMEM_LIMIT),
    )(dest, hft, xs0)


def _experts_kernel(bexp_ref, nvalid_ref, xs_ref, w1_ref, w3_ref, w2_ref, ys_ref, w1b, w3b, w2b, xb):
    i = pl.program_id(0)
    bm = xb.shape[0]
    prev = bexp_ref[jnp.maximum(i - 1, 0)]
    valid = i < nvalid_ref[0]

    @pl.when(valid & ((i == 0) | (bexp_ref[i] != prev)))
    def _():
        w1b[...] = w1_ref[0].astype(BF16)
        w3b[...] = w3_ref[0].astype(BF16)
        w2b[...] = w2_ref[0].astype(BF16)

    @pl.when(valid)
    def _():
        half = TOKEN_ROWS * LANES
        for j in range(TOKEN_ROWS):
            hi, lo = _load_token_pair(xs_ref, j, bm)
            xb[:, j * LANES:(j + 1) * LANES] = hi.astype(BF16)
            xb[:, half + j * LANES:half + (j + 1) * LANES] = lo.astype(BF16)
        x = xb[...]
        a1 = jnp.dot(x, w1b[...], preferred_element_type=F32)
        a3 = jnp.dot(x, w3b[...], preferred_element_type=F32)
        hmid = ((a1 * _sigmoid(a1)) * a3).astype(BF16)
        _store_token_tiles(ys_ref, jnp.dot(hmid, w2b[...], preferred_element_type=F32))


def _experts(block_exp, n_valid, xs, w1, w3, w2):
    bm = EXPERT_BLOCK
    n_blocks = xs.shape[0] // (bm * TOKEN_ROWS)
    E, D, DE = w1.shape
    assert D == 2 * TOKEN_ROWS * LANES
    clamp = lambda i, be, nv: jnp.minimum(i, nv[0] - 1)
    grid_spec = pltpu.PrefetchScalarGridSpec(
        num_scalar_prefetch=2,
        grid=(n_blocks,),
        in_specs=[
            pl.BlockSpec((bm * TOKEN_ROWS, LANES), lambda i, be, nv: (clamp(i, be, nv), 0)),
            pl.BlockSpec((1, D, DE), lambda i, be, nv: (be[clamp(i, be, nv)], 0, 0)),
            pl.BlockSpec((1, D, DE), lambda i, be, nv: (be[clamp(i, be, nv)], 0, 0)),
            pl.BlockSpec((1, DE, D), lambda i, be, nv: (be[clamp(i, be, nv)], 0, 0)),
        ],
        out_specs=pl.BlockSpec((bm * TOKEN_ROWS, LANES), lambda i, be, nv: (clamp(i, be, nv), 0)),
        scratch_shapes=[
            pltpu.VMEM((D, DE), BF16),
            pltpu.VMEM((D, DE), BF16),
            pltpu.VMEM((DE, D), BF16),
            pltpu.VMEM((bm, D), BF16),
        ],
    )
    return pl.pallas_call(
        _experts_kernel,
        name="experts",
        grid_spec=grid_spec,
        out_shape=jax.ShapeDtypeStruct(xs.shape, U32),
        input_output_aliases={2: 0},
        compiler_params=pltpu.CompilerParams(
            dimension_semantics=("arbitrary",), vmem_limit_bytes=VMEM_LIMIT),
    )(block_exp, n_valid, xs, w1, w3, w2)


def _combine_kernel(dest_ref, dest_next_ref, x1_ref, p_ref, info_ref, ys_ref, gple_ref, wgate_ref, bgate_ref,
                    wproj_ref, gfin_ref, o_ref, gbuf, sems, *, final_norm):
    tm = x1_ref.shape[0]
    i = pl.program_id(0)
    last = pl.num_programs(0) - 1
    slot = lax.rem(i, 2)

    def start_pair(idx_ref, t, s):
        _tile_copy(ys_ref, idx_ref[2 * t], gbuf.at[s, 0], t * TOKEN_ROWS, sems.at[s]).start(priority=0)
        _tile_copy(ys_ref, idx_ref[2 * t + 1], gbuf.at[s, 1], t * TOKEN_ROWS, sems.at[s]).start(priority=1)

    def wait_all(s):
        for _ in range(2 * tm):
            _tile_copy(ys_ref, 0, gbuf.at[s, 0], 0, sems.at[s]).wait()

    @pl.when(i == 0)
    def _():
        def body(t, c):
            start_pair(dest_ref, t, 0)
            return c
        lax.fori_loop(0, tm, body, 0, unroll=DMA_UNROLL)

    wait_all(slot)
    info = info_ref[...]
    w_a = info[:, 2:3]
    w_b = info[:, 3:4]
    his, los = [], []
    for j in range(TOKEN_ROWS):
        a_hi, a_lo = _load_token_pair(gbuf.at[slot, 0], j, tm)
        b_hi, b_lo = _load_token_pair(gbuf.at[slot, 1], j, tm)
        his.append(a_hi * w_a + b_hi * w_b)
        los.append(a_lo * w_a + b_lo * w_b)
    x2 = x1_ref[...] + jnp.concatenate(his + los, axis=1)

    for t in range(tm):
        start_pair(dest_next_ref, t, 1 - slot)

    pp = jnp.dot(p_ref[...].astype(BF16), wproj_ref[...], preferred_element_type=F32)
    gate = _sigmoid(jnp.dot(_rms(x2, gple_ref[...]).astype(BF16), wgate_ref[...], preferred_element_type=F32)
                    + bgate_ref[...])
    x3 = x2 + gate * pp
    o_ref[...] = _rms(x3, gfin_ref[...]) if final_norm else x3

    @pl.when(i == last)
    def _():
        wait_all(1 - slot)


def _combine(dest, x1, p, info, ys, g_ple, w_gate, b_gate, w_proj, g_final, final_norm):
    T, D = x1.shape
    PD = p.shape[1]
    tm = COMBINE_TILE
    assert T % tm == 0 and D == 2 * TOKEN_ROWS * LANES
    n_steps = T // tm
    const = lambda shape: pl.BlockSpec(shape, lambda i: (0,) * len(shape))
    return pl.pallas_call(
        functools.partial(_combine_kernel, final_norm=final_norm),
        name="combine",
        grid=(n_steps,),
        in_specs=[
            pl.BlockSpec((2 * tm,), lambda i: (i,), memory_space=pltpu.SMEM),
            pl.BlockSpec((2 * tm,), lambda i: (jnp.minimum(i + 1, n_steps - 1),), memory_space=pltpu.SMEM),
            pl.BlockSpec((tm, D), lambda i: (i, 0)),
            pl.BlockSpec((tm, PD), lambda i: (i, 0)),
            pl.BlockSpec((tm, LANES), lambda i: (i, 0)),
            pl.BlockSpec(memory_space=pl.ANY),
            const((1, D)),
            const((D, D)),
            const((1, D)),
            const((PD, D)),
            const((1, D)),
        ],
        out_specs=pl.BlockSpec((tm, D), lambda i: (i, 0)),
        out_shape=jax.ShapeDtypeStruct((T, D), F32),
        scratch_shapes=[pltpu.VMEM((2, 2, tm * TOKEN_ROWS, LANES), U32), pltpu.SemaphoreType.DMA((2,))],
        compiler_params=pltpu.CompilerParams(
            dimension_semantics=("arbitrary",), vmem_limit_bytes=VMEM_LIMIT),
    )(dest, dest, x1, p, info, ys, g_ple.reshape(1, D), w_gate.astype(BF16), b_gate.reshape(1, D),
      w_proj.astype(BF16), g_final.reshape(1, D))


def _slot_plan(info, counts_f, n_assign):
    bm = EXPERT_BLOCK
    counts = counts_f[0, :N_EXPERTS].astype(I32)
    padded = (counts + bm - 1) // bm * bm
    pad_end = jnp.cumsum(padded)
    pad_start = pad_end - padded
    n_blocks = -(-n_assign // bm) + N_EXPERTS
    block_start = jnp.arange(n_blocks, dtype=I32) * bm
    block_exp = jnp.minimum(jnp.sum((pad_end[None, :] <= block_start[:, None]).astype(I32), axis=1), N_EXPERTS - 1)
    n_valid = (pad_end[-1] // bm).astype(I32).reshape(1)
    e = info[:, 0:2].astype(I32)
    rank = info[:, 4:6].astype(I32)
    hit = e[:, :, None] == jnp.arange(N_EXPERTS, dtype=I32)[None, None, :]
    dest = (jnp.sum(jnp.where(hit, pad_start[None, None, :], 0), axis=-1) + rank).reshape(-1)
    return dest * TOKEN_ROWS, block_exp, n_valid, n_blocks * bm


def kernel(x, p, g_mix, w_in, conv_w, conv_b, lru_wa, lru_ba, lru_wx, lru_bx, lru_lambda, w_out, g_ffn, w_router_group, b_router_group, w_router_expert, b_router_expert, w1, w3, w2, g_ple, w_ple_gate, b_ple_gate, w_ple_proj, g_final):
    B, S, D = x.shape
    T = B * S
    depth = g_mix.shape[0]
    for l in range(depth):
        x1 = _mix(x, g_mix[l], w_in[l], conv_w[l], conv_b[l], lru_wa[l], lru_ba[l], lru_wx[l], lru_bx[l],
                  lru_lambda[l], w_out[l]).reshape(T, D)
        hft, info, counts = _route(x1, g_ffn[l], w_router_group[l], b_router_group[l],
                                   w_router_expert[l], b_router_expert[l])
        dest, block_exp, n_valid, n_slots = _slot_plan(info, counts, 2 * T)
        xs = _dispatch(dest, hft, n_slots)
        ys = _experts(block_exp, n_valid, xs, w1[l], w3[l], w2[l])
        x = _combine(dest, x1, p[l].reshape(T, -1), info, ys, g_ple[l], w_ple_gate[l], b_ple_gate[l],
                     w_ple_proj[l], g_final, l == depth - 1).reshape(B, S, D)
    return x
```

```python
import functools

import jax
import jax.numpy as jnp
from jax import lax
from jax.experimental import pallas as pl
from jax.experimental.pallas import tpu as pltpu

F32 = jnp.float32
BF16 = jnp.bfloat16
I32 = jnp.int32
U32 = jnp.uint32

EPS = 1e-6
LRU_C = 8.0
CONV_W = 4
RET_HEADS = 4
ROPE_BASE = 10000.0
N_GROUPS = 8
EXPERTS_PER_GROUP = 8
N_EXPERTS = N_GROUPS * EXPERTS_PER_GROUP

LANES = 128
SUBLANES = 8
VMEM_LIMIT = 56 * 1024 * 1024

MIX_TILE = 512
RET_CHUNK = 256
ROUTE_TILE = 512
DISPATCH_TILE = 512
EXPERT_BLOCK = 256
COMBINE_TILE = 256
DMA_UNROLL = 8
GATHER_SLOTS = 3


def _rms(x, g):
    ms = jnp.mean(x * x, axis=-1, keepdims=True)
    return (x * lax.rsqrt(ms + EPS)) * g


def _sigmoid(z):
    return 1.0 / (1.0 + jnp.exp(-z))


TOKEN_ROWS = 4


def _pack_pair(a, b):
    ab = lax.bitcast_convert_type(a.astype(BF16).astype(F32), U32)
    bb = lax.bitcast_convert_type(b.astype(BF16).astype(F32), U32)
    return ab | (bb >> 16)


def _unpack_pair(w):
    hi = lax.bitcast_convert_type(w & jnp.uint32(0xFFFF0000), F32)
    lo = lax.bitcast_convert_type(w << 16, F32)
    return hi, lo


def _store_token_tiles(tile_ref, x):
    n = x.shape[0]
    half = TOKEN_ROWS * LANES
    for j in range(TOKEN_ROWS):
        tile_ref[pl.ds(j, n, stride=TOKEN_ROWS), :] = _pack_pair(x[:, j * LANES:(j + 1) * LANES],
                                                                 x[:, half + j * LANES:half + (j + 1) * LANES])


def _load_token_pair(tile_ref, j, n):
    return _unpack_pair(tile_ref[pl.ds(j, n, stride=TOKEN_ROWS), :])


def _tile_copy(src_ref, src_row, dst_ref, dst_row, sem):
    return pltpu.make_async_copy(src_ref.at[pl.ds(pl.multiple_of(src_row, TOKEN_ROWS), TOKEN_ROWS), :],
                                 dst_ref.at[pl.ds(pl.multiple_of(dst_row, TOKEN_ROWS), TOKEN_ROWS), :], sem)


def _group_view(x):
    return x.reshape(x.shape[0] // SUBLANES, SUBLANES, x.shape[1])


def _shift_rows(x3, d, prev_group):
    r = pltpu.roll(x3, d, 1)
    prev = jnp.concatenate([pltpu.roll(prev_group[None], d, 1), r[:-1]], axis=0)
    sub = lax.broadcasted_iota(I32, x3.shape, 1)
    return jnp.where(sub < d, prev, r)


def _linear_scan(a, u, h0):
    a3, u3 = _group_view(a), _group_view(u)
    sub = lax.broadcasted_iota(I32, a3.shape, 1)
    d = 1
    while d < SUBLANES:
        keep = sub >= d
        a_sh = jnp.where(keep, pltpu.roll(a3, d, 1), 1.0)
        u_sh = jnp.where(keep, pltpu.roll(u3, d, 1), 0.0)
        u3 = u3 + a3 * u_sh
        a3 = a3 * a_sh
        d *= 2
    carry = h0
    groups = []
    for g in range(a3.shape[0]):
        hg = u3[g] + a3[g] * carry
        groups.append(hg)
        carry = hg[SUBLANES - 1:SUBLANES, :]
    return jnp.concatenate(groups, axis=0)


def _mix_kernel(x_ref, gmix_ref, win_ref, convw_ref, convb_ref, wg_ref, bg_ref, lam_ref,
                cos_ref, sin_ref, dmask_ref, qdec_ref, kdec_ref, cdec_ref, wout_ref,
                o_ref, xtail, qtail, hcar, state, ybuf):
    ts = x_ref.shape[1]
    d_lru = lam_ref.shape[1]
    d_ret = qdec_ref.shape[1]
    hd = d_ret // RET_HEADS
    chunk = dmask_ref.shape[1]
    half = d_lru // 2

    @pl.when(pl.program_id(1) == 0)
    def _():
        xtail[...] = jnp.zeros_like(xtail)
        qtail[...] = jnp.zeros_like(qtail)
        hcar[...] = jnp.zeros_like(hcar)
        state[...] = jnp.zeros_like(state)

    x = x_ref[0]
    h = _rms(x, gmix_ref[...]).astype(BF16)

    def proj(lo, width):
        return jnp.dot(h, win_ref[:, lo:lo + width], preferred_element_type=F32)

    xl = proj(0, d_lru)

    assert CONV_W == 4
    x3 = _group_view(xl)
    xm1 = _shift_rows(x3, 1, xtail[...])
    q3 = xm1 * convw_ref[0:1, :] + x3 * convw_ref[1:2, :]
    p3 = xm1 * convw_ref[2:3, :] + x3 * convw_ref[3:4, :]
    xc = ((convb_ref[...] + _shift_rows(q3, 2, qtail[...])) + p3).reshape(ts, d_lru)
    xtail[...] = x3[-1]
    qtail[...] = q3[-1]

    xcb = xc.astype(BF16)
    g0 = jnp.dot(xcb[:, :half], wg_ref[0], preferred_element_type=F32)
    g1 = jnp.dot(xcb[:, half:], wg_ref[1], preferred_element_type=F32)
    ra = jnp.concatenate([g0[:, :half], g1[:, :half]], axis=1) + bg_ref[:, :d_lru]
    ix = jnp.concatenate([g0[:, half:], g1[:, half:]], axis=1) + bg_ref[:, d_lru:]
    r = _sigmoid(ra)
    i_gate = _sigmoid(ix)
    z = -lam_ref[...]
    softplus = jnp.maximum(z, 0.0) + jnp.log1p(jnp.exp(-jnp.abs(z)))
    log_a = (-LRU_C * r) * softplus
    a = jnp.exp(log_a)
    v = 1.0 - a * a
    u = jnp.where(v > 0.0, v * lax.rsqrt(v), 0.0) * (i_gate * xc)
    hseq = _linear_scan(a, u, hcar[0:1, :])
    hcar[0:1, :] = hseq[ts - 1:ts, :]
    gl = proj(d_lru, d_lru)
    ybuf[:, 0:d_lru] = (hseq * jax.nn.gelu(gl)).astype(BF16)

    base = 2 * d_lru
    q = proj(base, d_ret)
    k = proj(base + d_ret, d_ret)
    v = proj(base + 2 * d_ret, d_ret)
    gr = proj(base + 3 * d_ret, d_ret)
    scale = hd ** -0.5
    for hh in range(RET_HEADS):
        ls = slice(hh * hd, (hh + 1) * hd)
        for c in range(ts // chunk):
            rs = slice(c * chunk, (c + 1) * chunk)
            cosv = cos_ref[rs, :]
            sinv = sin_ref[rs, :]
            qh = q[rs, ls]
            kh = k[rs, ls]
            qr = qh * cosv + pltpu.roll(qh, hd // 2, 1) * sinv
            kr = (kh * cosv + pltpu.roll(kh, hd // 2, 1) * sinv) * scale
            vb = v[rs, ls].astype(BF16)
            scores = lax.dot_general(qr.astype(BF16), kr.astype(BF16), (((1,), (1,)), ((), ())),
                                     preferred_element_type=F32)
            inner = jnp.dot((scores * dmask_ref[hh]).astype(BF16), vb, preferred_element_type=F32)
            st = state[hh]
            cross = jnp.dot((qr * qdec_ref[:, ls]).astype(BF16), st.astype(BF16), preferred_element_type=F32)
            kv = lax.dot_general((kr * kdec_ref[:, ls]).astype(BF16), vb, (((0,), (0,)), ((), ())),
                                 preferred_element_type=F32)
            state[hh] = st * cdec_ref[:, ls] + kv
            o = inner + cross
            mu = jnp.mean(o, axis=-1, keepdims=True)
            oc = o - mu
            var = jnp.mean(oc * oc, axis=-1, keepdims=True)
            on = oc * lax.rsqrt(var + EPS)
            grh = gr[rs, ls]
            ybuf[rs, d_lru + hh * hd:d_lru + (hh + 1) * hd] = ((grh * _sigmoid(grh)) * on).astype(BF16)

    o_ref[0] = x + jnp.dot(ybuf[...], wout_ref[...], preferred_element_type=F32)


def _mix(x, g_mix, w_in, conv_w, conv_b, lru_wa, lru_ba, lru_wx, lru_bx, lam, w_out):
    B, S, D = x.shape
    d_lru = lam.shape[0]
    d_ret = (w_in.shape[1] - 2 * d_lru) // 4
    hd = d_ret // RET_HEADS
    ts, chunk = MIX_TILE, RET_CHUNK
    assert S % ts == 0 and ts % chunk == 0 and d_lru % (2 * LANES) == 0 and hd == LANES
    nb, bw = lru_wa.shape[0], lru_wa.shape[1]
    half = d_lru // 2
    per_half = nb // 2

    def blockdiag(w):
        out = jnp.zeros((half, half), F32)
        for j in range(per_half):
            out = out.at[j * bw:(j + 1) * bw, j * bw:(j + 1) * bw].set(w[j])
        return out
    wg = jnp.stack([jnp.concatenate([blockdiag(lru_wa[hf * per_half:(hf + 1) * per_half]),
                                     blockdiag(lru_wx[hf * per_half:(hf + 1) * per_half])], axis=1)
                    for hf in range(2)]).astype(BF16)
    bg = jnp.concatenate([lru_ba.reshape(1, d_lru), lru_bx.reshape(1, d_lru)], axis=1)

    inv = ROPE_BASE ** (-jnp.arange(hd // 2, dtype=F32) / (hd // 2))
    ang = jnp.arange(S, dtype=F32)[:, None] * inv[None, :]
    cos2 = jnp.concatenate([jnp.cos(ang), jnp.cos(ang)], axis=1)
    sin2 = jnp.concatenate([-jnp.sin(ang), jnp.sin(ang)], axis=1)
    log_g = jnp.log(1.0 - 2.0 ** (-5.0 - jnp.arange(RET_HEADS, dtype=F32)))
    idx = jnp.arange(chunk, dtype=F32)
    diff = idx[:, None] - idx[None, :]
    dmask = jnp.where(diff >= 0, jnp.exp(jnp.maximum(diff, 0.0)[None] * log_g[:, None, None]), 0.0)
    rep = lambda t: jnp.repeat(t, hd, axis=-1)
    qdec = rep(jnp.exp((idx + 1.0)[:, None] * log_g[None, :]))
    kdec = rep(jnp.exp((chunk - 1.0 - idx)[:, None] * log_g[None, :]))
    cdec = rep(jnp.exp(chunk * log_g)[None, :])

    full = lambda shape: pl.BlockSpec(shape, lambda b, s: (0,) * len(shape))
    return pl.pallas_call(
        _mix_kernel,
        name="mix",
        grid=(B, S // ts),
        in_specs=[
            pl.BlockSpec((1, ts, D), lambda b, s: (b, s, 0)),
            full((1, D)),
            full(w_in.shape),
            full((CONV_W, d_lru)),
            full((1, d_lru)),
            full(wg.shape),
            full(bg.shape),
            full((1, d_lru)),
            pl.BlockSpec((ts, hd), lambda b, s: (s, 0)),
            pl.BlockSpec((ts, hd), lambda b, s: (s, 0)),
            full(dmask.shape),
            full(qdec.shape),
            full(kdec.shape),
            full(cdec.shape),
            full(w_out.shape),
        ],
        out_specs=pl.BlockSpec((1, ts, D), lambda b, s: (b, s, 0)),
        out_shape=jax.ShapeDtypeStruct((B, S, D), F32),
        scratch_shapes=[
            pltpu.VMEM((SUBLANES, d_lru), F32),
            pltpu.VMEM((SUBLANES, d_lru), F32),
            pltpu.VMEM((SUBLANES, d_lru), F32),
            pltpu.VMEM((RET_HEADS, hd, hd), F32),
            pltpu.VMEM((ts, d_lru + d_ret), BF16),
        ],
        compiler_params=pltpu.CompilerParams(
            dimension_semantics=("arbitrary", "arbitrary"), vmem_limit_bytes=VMEM_LIMIT),
    )(x, g_mix.reshape(1, D), w_in.astype(BF16), conv_w, conv_b.reshape(1, d_lru), wg, bg,
      lam.reshape(1, d_lru), cos2, sin2, dmask, qdec, kdec, cdec, w_out.astype(BF16))


def _route_kernel(x_ref, g_ref, wr_ref, br_ref, hft_ref, info_ref, cnt_ref, carry):
    tm, D = x_ref.shape

    @pl.when(pl.program_id(0) == 0)
    def _():
        carry[...] = jnp.zeros_like(carry)

    hf = _rms(x_ref[...], g_ref[...])
    _store_token_tiles(hft_ref, hf)

    h_hi = hf.astype(BF16)
    h_lo = (hf - h_hi.astype(F32)).astype(BF16)
    logits = (jnp.dot(h_hi, wr_ref[0], preferred_element_type=F32)
              + (jnp.dot(h_hi, wr_ref[1], preferred_element_type=F32)
                 + jnp.dot(h_lo, wr_ref[0], preferred_element_type=F32))) + br_ref[...]
    lane = lax.broadcasted_iota(I32, logits.shape, 1)
    neg = jnp.float32(-jnp.inf)
    gmask = lane < N_GROUPS
    gl = jnp.where(gmask, logits, neg)
    gmax = jnp.max(gl, axis=-1, keepdims=True)
    g_sel = jnp.min(jnp.where(gl == gmax, lane, LANES), axis=-1, keepdims=True)
    g_w = 1.0 / jnp.sum(jnp.where(gmask, jnp.exp(logits - gmax), 0.0), axis=-1, keepdims=True)

    e_lane = lane - N_GROUPS
    emask = (e_lane >= 0) & (e_lane < N_EXPERTS) & ((e_lane >> 3) == g_sel)
    el = jnp.where(emask, logits, neg)
    v1 = jnp.max(el, axis=-1, keepdims=True)
    i1 = jnp.min(jnp.where(el == v1, e_lane, LANES), axis=-1, keepdims=True)
    el2 = jnp.where(e_lane == i1, neg, el)
    v2 = jnp.max(el2, axis=-1, keepdims=True)
    i2 = jnp.min(jnp.where(el2 == v2, e_lane, LANES), axis=-1, keepdims=True)
    t = jnp.exp(v2 - v1)
    w1 = g_w / (1.0 + t)
    w2 = g_w * (t / (1.0 + t))

    hit = (lane == i1) | (lane == i2)
    onehot = jnp.where(hit, 1.0, 0.0)
    rr = lax.broadcasted_iota(I32, (tm, tm), 0)
    cc = lax.broadcasted_iota(I32, (tm, tm), 1)
    tri = jnp.where(rr > cc, 1.0, 0.0).astype(BF16)
    prefix = jnp.dot(tri, onehot.astype(BF16), preferred_element_type=F32) + carry[0:1, :]
    r1 = jnp.sum(jnp.where(lane == i1, prefix, 0.0), axis=-1, keepdims=True)
    r2 = jnp.sum(jnp.where(lane == i2, prefix, 0.0), axis=-1, keepdims=True)
    carry[0:1, :] = carry[0:1, :] + jnp.sum(onehot, axis=0, keepdims=True)

    info = jnp.where(lane == 0, i1.astype(F32), 0.0)
    info = jnp.where(lane == 1, i2.astype(F32), info)
    info = jnp.where(lane == 2, w1, info)
    info = jnp.where(lane == 3, w2, info)
    info = jnp.where(lane == 4, r1, info)
    info = jnp.where(lane == 5, r2, info)
    info_ref[...] = info
    cnt_ref[...] = carry[...]


def _route(x1, g_ffn, w_grp, b_grp, w_exp, b_exp):
    T, D = x1.shape
    tm = ROUTE_TILE
    assert T % tm == 0 and N_GROUPS + N_EXPERTS <= LANES and T < (1 << 24) and D == 2 * TOKEN_ROWS * LANES
    pad = LANES - N_GROUPS - N_EXPERTS
    wr = jnp.concatenate([w_grp, w_exp, jnp.zeros((D, pad), F32)], axis=1)
    wr_hi = wr.astype(BF16)
    wr = jnp.stack([wr_hi, (wr - wr_hi.astype(F32)).astype(BF16)])
    br =jnp.concatenate([b_grp, b_exp, jnp.zeros((pad,), F32)]).reshape(1, LANES)
    return pl.pallas_call(
        _route_kernel,
        name="route",
        grid=(T // tm,),
        in_specs=[
            pl.BlockSpec((tm, D), lambda i: (i, 0)),
            pl.BlockSpec((1, D), lambda i: (0, 0)),
            pl.BlockSpec((2, D, LANES), lambda i: (0, 0, 0)),
            pl.BlockSpec((1, LANES), lambda i: (0, 0)),
        ],
        out_specs=[
            pl.BlockSpec((tm * TOKEN_ROWS, LANES), lambda i: (i, 0)),
            pl.BlockSpec((tm, LANES), lambda i: (i, 0)),
            pl.BlockSpec((SUBLANES, LANES), lambda i: (0, 0)),
        ],
        out_shape=[
            jax.ShapeDtypeStruct((T * TOKEN_ROWS, LANES), U32),
            jax.ShapeDtypeStruct((T, LANES), F32),
            jax.ShapeDtypeStruct((SUBLANES, LANES), F32),
        ],
        scratch_shapes=[pltpu.VMEM((SUBLANES, LANES), F32)],
        compiler_params=pltpu.CompilerParams(
            dimension_semantics=("arbitrary",), vmem_limit_bytes=VMEM_LIMIT),
    )(x1, g_ffn.reshape(1, D), wr, br)


def _dispatch_kernel(dest_ref, hft_ref, xs_in_ref, xs_ref, sem):
    del xs_in_ref
    tm = hft_ref.shape[0] // TOKEN_ROWS

    def start(t, c):
        _tile_copy(hft_ref, t * TOKEN_ROWS, xs_ref, dest_ref[2 * t], sem).start(priority=0)
        _tile_copy(hft_ref, t * TOKEN_ROWS, xs_ref, dest_ref[2 * t + 1], sem).start(priority=1)
        return c

    lax.fori_loop(0, tm, start, 0, unroll=DMA_UNROLL)

    for _ in range(2 * tm):
        _tile_copy(hft_ref, 0, xs_ref, 0, sem).wait()


def _dispatch(dest, hft, n_slots):
    T = hft.shape[0] // TOKEN_ROWS
    tm = DISPATCH_TILE
    assert T % tm == 0
    xs0 = jnp.zeros((n_slots * TOKEN_ROWS, LANES), U32)
    return pl.pallas_call(
        _dispatch_kernel,
        name="dispatch",
        grid=(T // tm,),
        in_specs=[
            pl.BlockSpec((2 * tm,), lambda i: (i,), memory_space=pltpu.SMEM),
            pl.BlockSpec((tm * TOKEN_ROWS, LANES), lambda i: (i, 0)),
            pl.BlockSpec(memory_space=pl.ANY),
        ],
        out_specs=pl.BlockSpec(memory_space=pl.ANY),
        out_shape=jax.ShapeDtypeStruct(xs0.shape, U32),
        scratch_shapes=[pltpu.SemaphoreType.DMA],
        input_output_aliases={2: 0},
        compiler_params=pltpu.CompilerParams(
            dimension_semantics=("arbitrary",), has_side_effects=True, vmem_limit_bytes=VMEM_LIMIT),
    )(dest, hft, xs0)


def _experts_kernel(bexp_ref, nvalid_ref, xs_ref, w1_ref, w3_ref, w2_ref, ys_ref, w1b, w3b, w2b, xb):
    i = pl.program_id(0)
    bm = xb.shape[0]
    prev = bexp_ref[jnp.maximum(i - 1, 0)]
    valid = i < nvalid_ref[0]

    @pl.when(valid & ((i == 0) | (bexp_ref[i] != prev)))
    def _():
        w1b[...] = w1_ref[0].astype(BF16)
        w3b[...] = w3_ref[0].astype(BF16)
        w2b[...] = w2_ref[0].astype(BF16)

    @pl.when(valid)
    def _():
        half = TOKEN_ROWS * LANES
        for j in range(TOKEN_ROWS):
            hi, lo = _load_token_pair(xs_ref, j, bm)
            xb[:, j * LANES:(j + 1) * LANES] = hi.astype(BF16)
            xb[:, half + j * LANES:half + (j + 1) * LANES] = lo.astype(BF16)
        x = xb[...]
        a1 = jnp.dot(x, w1b[...], preferred_element_type=F32)
        a3 = jnp.dot(x, w3b[...], preferred_element_type=F32)
        hmid = ((a1 * _sigmoid(a1)) * a3).astype(BF16)
        _store_token_tiles(ys_ref, jnp.dot(hmid, w2b[...], preferred_element_type=F32))


def _experts(block_exp, n_valid, xs, w1, w3, w2):
    bm = EXPERT_BLOCK
    n_blocks = xs.shape[0] // (bm * TOKEN_ROWS)
    E, D, DE = w1.shape
    assert D == 2 * TOKEN_ROWS * LANES
    clamp = lambda i, be, nv: jnp.minimum(i, nv[0] - 1)
    grid_spec = pltpu.PrefetchScalarGridSpec(
        num_scalar_prefetch=2,
        grid=(n_blocks,),
        in_specs=[
            pl.BlockSpec((bm * TOKEN_ROWS, LANES), lambda i, be, nv: (clamp(i, be, nv), 0)),
            pl.BlockSpec((1, D, DE), lambda i, be, nv: (be[clamp(i, be, nv)], 0, 0)),
            pl.BlockSpec((1, D, DE), lambda i, be, nv: (be[clamp(i, be, nv)], 0, 0)),
            pl.BlockSpec((1, DE, D), lambda i, be, nv: (be[clamp(i, be, nv)], 0, 0)),
        ],
        out_specs=pl.BlockSpec((bm * TOKEN_ROWS, LANES), lambda i, be, nv: (clamp(i, be, nv), 0)),
        scratch_shapes=[
            pltpu.VMEM((D, DE), BF16),
            pltpu.VMEM((D, DE), BF16),
            pltpu.VMEM((DE, D), BF16),
            pltpu.VMEM((bm, D), BF16),
        ],
    )
    return pl.pallas_call(
        _experts_kernel,
        name="experts",
        grid_spec=grid_spec,
        out_shape=jax.ShapeDtypeStruct(xs.shape, U32),
        input_output_aliases={2: 0},
        compiler_params=pltpu.CompilerParams(
            dimension_semantics=("arbitrary",), vmem_limit_bytes=VMEM_LIMIT),
    )(block_exp, n_valid, xs, w1, w3, w2)


def _combine_kernel(dest0_ref, dest1_ref, dest2_ref, x1_ref, p_ref, info_ref, ys_ref, gple_ref, wgate_ref, bgate_ref,
                    wproj_ref, gfin_ref, o_ref, gbuf, sems, *, final_norm):
    tm = x1_ref.shape[0]
    i = pl.program_id(0)
    last = pl.num_programs(0) - 1
    slot = lax.rem(i, GATHER_SLOTS)
    ahead = lax.rem(i + GATHER_SLOTS - 1, GATHER_SLOTS)

    def start_pair(idx_ref, t, s):
        _tile_copy(ys_ref, idx_ref[2 * t], gbuf.at[s, 0], t * TOKEN_ROWS, sems.at[s]).start(priority=0)
        _tile_copy(ys_ref, idx_ref[2 * t + 1], gbuf.at[s, 1], t * TOKEN_ROWS, sems.at[s]).start(priority=1)

    def wait_all(s):
        for _ in range(2 * tm):
            _tile_copy(ys_ref, 0, gbuf.at[s, 0], 0, sems.at[s]).wait()

    @pl.when(i == 0)
    def _():
        def body(t, c):
            start_pair(dest0_ref, t, 0)
            start_pair(dest1_ref, t, 1)
            return c
        lax.fori_loop(0, tm, body, 0, unroll=DMA_UNROLL)

    wait_all(slot)
    info = info_ref[...]
    w_a = info[:, 2:3]
    w_b = info[:, 3:4]
    his, los = [], []
    for j in range(TOKEN_ROWS):
        a_hi, a_lo = _load_token_pair(gbuf.at[slot, 0], j, tm)
        b_hi, b_lo = _load_token_pair(gbuf.at[slot, 1], j, tm)
        his.append(a_hi * w_a + b_hi * w_b)
        los.append(a_lo * w_a + b_lo * w_b)
    x2 = x1_ref[...] + jnp.concatenate(his + los, axis=1)

    for t in range(tm):
        start_pair(dest2_ref, t, ahead)

    pp = jnp.dot(p_ref[...].astype(BF16), wproj_ref[...], preferred_element_type=F32)
    gate = _sigmoid(jnp.dot(_rms(x2, gple_ref[...]).astype(BF16), wgate_ref[...], preferred_element_type=F32)
                    + bgate_ref[...])
    x3 = x2 + gate * pp
    o_ref[...] = _rms(x3, gfin_ref[...]) if final_norm else x3

    @pl.when(i == last)
    def _():
        wait_all(lax.rem(i + 1, GATHER_SLOTS))
        wait_all(ahead)


def _combine(dest, x1, p, info, ys, g_ple, w_gate, b_gate, w_proj, g_final, final_norm):
    T, D = x1.shape
    PD = p.shape[1]
    tm = COMBINE_TILE
    assert T % tm == 0 and D == 2 * TOKEN_ROWS * LANES
    n_steps = T // tm
    const = lambda shape: pl.BlockSpec(shape, lambda i: (0,) * len(shape))
    return pl.pallas_call(
        functools.partial(_combine_kernel, final_norm=final_norm),
        name="combine",
        grid=(n_steps,),
        in_specs=[
            pl.BlockSpec((2 * tm,), lambda i: (i,), memory_space=pltpu.SMEM),
            pl.BlockSpec((2 * tm,), lambda i: (jnp.minimum(i + 1, n_steps - 1),), memory_space=pltpu.SMEM),
            pl.BlockSpec((2 * tm,), lambda i: (jnp.minimum(i + 2, n_steps - 1),), memory_space=pltpu.SMEM),
            pl.BlockSpec((tm, D), lambda i: (i, 0)),
            pl.BlockSpec((tm, PD), lambda i: (i, 0)),
            pl.BlockSpec((tm, LANES), lambda i: (i, 0)),
            pl.BlockSpec(memory_space=pl.ANY),
            const((1, D)),
            const((D, D)),
            const((1, D)),
            const((PD, D)),
            const((1, D)),
        ],
        out_specs=pl.BlockSpec((tm, D), lambda i: (i, 0)),
        out_shape=jax.ShapeDtypeStruct((T, D), F32),
        scratch_shapes=[pltpu.VMEM((GATHER_SLOTS, 2, tm * TOKEN_ROWS, LANES), U32),
                        pltpu.SemaphoreType.DMA((GATHER_SLOTS,))],
        compiler_params=pltpu.CompilerParams(
            dimension_semantics=("arbitrary",), vmem_limit_bytes=VMEM_LIMIT),
    )(dest, dest, dest, x1, p, info, ys, g_ple.reshape(1, D), w_gate.astype(BF16), b_gate.reshape(1, D),
      w_proj.astype(BF16), g_final.reshape(1, D))


def _slot_plan(info, counts_f, n_assign):
    bm = EXPERT_BLOCK
    counts = counts_f[0, :N_EXPERTS].astype(I32)
    padded = (counts + bm - 1) // bm * bm
    pad_end = jnp.cumsum(padded)
    pad_start = pad_end - padded
    n_blocks = -(-n_assign // bm) + N_EXPERTS
    block_start = jnp.arange(n_blocks, dtype=I32) * bm
    block_exp = jnp.minimum(jnp.sum((pad_end[None, :] <= block_start[:, None]).astype(I32), axis=1), N_EXPERTS - 1)
    n_valid = (pad_end[-1] // bm).astype(I32).reshape(1)
    e = info[:, 0:2].astype(I32)
    rank = info[:, 4:6].astype(I32)
    hit = e[:, :, None] == jnp.arange(N_EXPERTS, dtype=I32)[None, None, :]
    dest = (jnp.sum(jnp.where(hit, pad_start[None, None, :], 0), axis=-1) + rank).reshape(-1)
    return dest * TOKEN_ROWS, block_exp, n_valid, n_blocks * bm


def kernel(x, p, g_mix, w_in, conv_w, conv_b, lru_wa, lru_ba, lru_wx, lru_bx, lru_lambda, w_out, g_ffn, w_router_group, b_router_group, w_router_expert, b_router_expert, w1, w3, w2, g_ple, w_ple_gate, b_ple_gate, w_ple_proj, g_final):
    B, S, D = x.shape
    T = B * S
    depth = g_mix.shape[0]
    for l in range(depth):
        x1 = _mix(x, g_mix[l], w_in[l], conv_w[l], conv_b[l], lru_wa[l], lru_ba[l], lru_wx[l], lru_bx[l],
                  lru_lambda[l], w_out[l]).reshape(T, D)
        hft, info, counts = _route(x1, g_ffn[l], w_router_group[l], b_router_group[l],
                                   w_router_expert[l], b_router_expert[l])
        dest, block_exp, n_valid, n_slots = _slot_plan(info, counts, 2 * T)
        xs = _dispatch(dest, hft, n_slots)
        ys = _experts(block_exp, n_valid, xs, w1[l], w3[l], w2[l])
        x = _combine(dest, x1, p[l].reshape(T, -1), info, ys, g_ple[l], w_ple_gate[l], b_ple_gate[l],
                     w_ple_proj[l], g_final, l == depth - 1).reshape(B, S, D)
    return x
```

```python
import functools

import jax
import jax.numpy as jnp
from jax import lax
from jax.experimental import pallas as pl
from jax.experimental.pallas import tpu as pltpu

F32 = jnp.float32
BF16 = jnp.bfloat16
I32 = jnp.int32
U32 = jnp.uint32

EPS = 1e-6
LRU_C = 8.0
CONV_W = 4
RET_HEADS = 4
ROPE_BASE = 10000.0
N_GROUPS = 8
EXPERTS_PER_GROUP = 8
N_EXPERTS = N_GROUPS * EXPERTS_PER_GROUP

LANES = 128
SUBLANES = 8
VMEM_LIMIT = 56 * 1024 * 1024

MIX_TILE = 512
RET_CHUNK = 256
ROUTE_TILE = 512
DISPATCH_TILE = 512
EXPERT_BLOCK = 256
COMBINE_TILE = 256
DMA_UNROLL = 8
GATHER_SLOTS = 3


def _rms(x, g):
    ms = jnp.mean(x * x, axis=-1, keepdims=True)
    return (x * lax.rsqrt(ms + EPS)) * g


def _sigmoid(z):
    return 1.0 / (1.0 + jnp.exp(-z))


TOKEN_ROWS = 4


def _pack_pair(a, b):
    ab = lax.bitcast_convert_type(a.astype(BF16).astype(F32), U32)
    bb = lax.bitcast_convert_type(b.astype(BF16).astype(F32), U32)
    return ab | (bb >> 16)


def _unpack_pair(w):
    hi = lax.bitcast_convert_type(w & jnp.uint32(0xFFFF0000), F32)
    lo = lax.bitcast_convert_type(w << 16, F32)
    return hi, lo


def _store_token_tiles(tile_ref, x):
    n = x.shape[0]
    half = TOKEN_ROWS * LANES
    for j in range(TOKEN_ROWS):
        tile_ref[pl.ds(j, n, stride=TOKEN_ROWS), :] = _pack_pair(x[:, j * LANES:(j + 1) * LANES],
                                                                 x[:, half + j * LANES:half + (j + 1) * LANES])


def _load_token_pair(tile_ref, j, n):
    return _unpack_pair(tile_ref[pl.ds(j, n, stride=TOKEN_ROWS), :])


def _tile_copy(src_ref, src_row, dst_ref, dst_row, sem):
    return pltpu.make_async_copy(src_ref.at[pl.ds(pl.multiple_of(src_row, TOKEN_ROWS), TOKEN_ROWS), :],
                                 dst_ref.at[pl.ds(pl.multiple_of(dst_row, TOKEN_ROWS), TOKEN_ROWS), :], sem)


def _group_view(x):
    return x.reshape(x.shape[0] // SUBLANES, SUBLANES, x.shape[1])


def _shift_rows(x3, d, prev_group):
    r = pltpu.roll(x3, d, 1)
    prev = jnp.concatenate([pltpu.roll(prev_group[None], d, 1), r[:-1]], axis=0)
    sub = lax.broadcasted_iota(I32, x3.shape, 1)
    return jnp.where(sub < d, prev, r)


def _linear_scan(a, u, h0):
    a3, u3 = _group_view(a), _group_view(u)
    sub = lax.broadcasted_iota(I32, a3.shape, 1)
    d = 1
    while d < SUBLANES:
        keep = sub >= d
        a_sh = jnp.where(keep, pltpu.roll(a3, d, 1), 1.0)
        u_sh = jnp.where(keep, pltpu.roll(u3, d, 1), 0.0)
        u3 = u3 + a3 * u_sh
        a3 = a3 * a_sh
        d *= 2
    carry = h0
    groups = []
    for g in range(a3.shape[0]):
        hg = u3[g] + a3[g] * carry
        groups.append(hg)
        carry = hg[SUBLANES - 1:SUBLANES, :]
    return jnp.concatenate(groups, axis=0)


def _mix_kernel(x_ref, gmix_ref, win_ref, convw_ref, convb_ref, wg_ref, bg_ref, lam_ref,
                cos_ref, sin_ref, dmask_ref, qdec_ref, kdec_ref, cdec_ref, wout_ref,
                o_ref, xtail, qtail, hcar, state, ybuf):
    ts = x_ref.shape[1]
    d_lru = lam_ref.shape[1]
    d_ret = qdec_ref.shape[1]
    hd = d_ret // RET_HEADS
    chunk = dmask_ref.shape[1]
    half = d_lru // 2

    @pl.when(pl.program_id(1) == 0)
    def _():
        xtail[...] = jnp.zeros_like(xtail)
        qtail[...] = jnp.zeros_like(qtail)
        hcar[...] = jnp.zeros_like(hcar)
        state[...] = jnp.zeros_like(state)

    x = x_ref[0]
    h = _rms(x, gmix_ref[...]).astype(BF16)

    def proj(lo, width):
        return jnp.dot(h, win_ref[:, lo:lo + width], preferred_element_type=F32)

    xl = proj(0, d_lru)

    assert CONV_W == 4
    x3 = _group_view(xl)
    xm1 = _shift_rows(x3, 1, xtail[...])
    q3 = xm1 * convw_ref[0:1, :] + x3 * convw_ref[1:2, :]
    p3 = xm1 * convw_ref[2:3, :] + x3 * convw_ref[3:4, :]
    xc = ((convb_ref[...] + _shift_rows(q3, 2, qtail[...])) + p3).reshape(ts, d_lru)
    xtail[...] = x3[-1]
    qtail[...] = q3[-1]

    xcb = xc.astype(BF16)
    g0 = jnp.dot(xcb[:, :half], wg_ref[0], preferred_element_type=F32)
    g1 = jnp.dot(xcb[:, half:], wg_ref[1], preferred_element_type=F32)
    ra = jnp.concatenate([g0[:, :half], g1[:, :half]], axis=1) + bg_ref[:, :d_lru]
    ix = jnp.concatenate([g0[:, half:], g1[:, half:]], axis=1) + bg_ref[:, d_lru:]
    r = _sigmoid(ra)
    i_gate = _sigmoid(ix)
    z = -lam_ref[...]
    softplus = jnp.maximum(z, 0.0) + jnp.log1p(jnp.exp(-jnp.abs(z)))
    log_a = (-LRU_C * r) * softplus
    a = jnp.exp(log_a)
    v = 1.0 - a * a
    u = jnp.where(v > 0.0, v * lax.rsqrt(v), 0.0) * (i_gate * xc)
    hseq = _linear_scan(a, u, hcar[0:1, :])
    hcar[0:1, :] = hseq[ts - 1:ts, :]
    gl = proj(d_lru, d_lru)
    ybuf[:, 0:d_lru] = (hseq * jax.nn.gelu(gl)).astype(BF16)

    base = 2 * d_lru
    q = proj(base, d_ret)
    k = proj(base + d_ret, d_ret)
    v = proj(base + 2 * d_ret, d_ret)
    gr = proj(base + 3 * d_ret, d_ret)
    scale = hd ** -0.5
    for hh in range(RET_HEADS):
        ls = slice(hh * hd, (hh + 1) * hd)
        for c in range(ts // chunk):
            rs = slice(c * chunk, (c + 1) * chunk)
            cosv = cos_ref[rs, :]
            sinv = sin_ref[rs, :]
            qh = q[rs, ls]
            kh = k[rs, ls]
            qr = qh * cosv + pltpu.roll(qh, hd // 2, 1) * sinv
            kr = (kh * cosv + pltpu.roll(kh, hd // 2, 1) * sinv) * scale
            vb = v[rs, ls].astype(BF16)
            scores = lax.dot_general(qr.astype(BF16), kr.astype(BF16), (((1,), (1,)), ((), ())),
                                     preferred_element_type=F32)
            inner = jnp.dot((scores * dmask_ref[hh]).astype(BF16), vb, preferred_element_type=F32)
            st = state[hh]
            cross = jnp.dot((qr * qdec_ref[:, ls]).astype(BF16), st.astype(BF16), preferred_element_type=F32)
            kv = lax.dot_general((kr * kdec_ref[:, ls]).astype(BF16), vb, (((0,), (0,)), ((), ())),
                                 preferred_element_type=F32)
            state[hh] = st * cdec_ref[:, ls] + kv
            o = inner + cross
            mu = jnp.mean(o, axis=-1, keepdims=True)
            oc = o - mu
            var = jnp.mean(oc * oc, axis=-1, keepdims=True)
            on = oc * lax.rsqrt(var + EPS)
            grh = gr[rs, ls]
            ybuf[rs, d_lru + hh * hd:d_lru + (hh + 1) * hd] = ((grh * _sigmoid(grh)) * on).astype(BF16)

    o_ref[0] = x + jnp.dot(ybuf[...], wout_ref[...], preferred_element_type=F32)


def _mix(x, g_mix, w_in, conv_w, conv_b, lru_wa, lru_ba, lru_wx, lru_bx, lam, w_out):
    B, S, D = x.shape
    d_lru = lam.shape[0]
    d_ret = (w_in.shape[1] - 2 * d_lru) // 4
    hd = d_ret // RET_HEADS
    ts, chunk = MIX_TILE, RET_CHUNK
    assert S % ts == 0 and ts % chunk == 0 and d_lru % (2 * LANES) == 0 and hd == LANES
    nb, bw = lru_wa.shape[0], lru_wa.shape[1]
    half = d_lru // 2
    per_half = nb // 2

    def blockdiag(w):
        out = jnp.zeros((half, half), F32)
        for j in range(per_half):
            out = out.at[j * bw:(j + 1) * bw, j * bw:(j + 1) * bw].set(w[j])
        return out
    wg = jnp.stack([jnp.concatenate([blockdiag(lru_wa[hf * per_half:(hf + 1) * per_half]),
                                     blockdiag(lru_wx[hf * per_half:(hf + 1) * per_half])], axis=1)
                    for hf in range(2)]).astype(BF16)
    bg = jnp.concatenate([lru_ba.reshape(1, d_lru), lru_bx.reshape(1, d_lru)], axis=1)

    inv = ROPE_BASE ** (-jnp.arange(hd // 2, dtype=F32) / (hd // 2))
    ang = jnp.arange(S, dtype=F32)[:, None] * inv[None, :]
    cos2 = jnp.concatenate([jnp.cos(ang), jnp.cos(ang)], axis=1)
    sin2 = jnp.concatenate([-jnp.sin(ang), jnp.sin(ang)], axis=1)
    log_g = jnp.log(1.0 - 2.0 ** (-5.0 - jnp.arange(RET_HEADS, dtype=F32)))
    idx = jnp.arange(chunk, dtype=F32)
    diff = idx[:, None] - idx[None, :]
    dmask = jnp.where(diff >= 0, jnp.exp(jnp.maximum(diff, 0.0)[None] * log_g[:, None, None]), 0.0)
    rep = lambda t: jnp.repeat(t, hd, axis=-1)
    qdec = rep(jnp.exp((idx + 1.0)[:, None] * log_g[None, :]))
    kdec = rep(jnp.exp((chunk - 1.0 - idx)[:, None] * log_g[None, :]))
    cdec = rep(jnp.exp(chunk * log_g)[None, :])

    full = lambda shape: pl.BlockSpec(shape, lambda b, s: (0,) * len(shape))
    return pl.pallas_call(
        _mix_kernel,
        name="mix",
        grid=(B, S // ts),
        in_specs=[
            pl.BlockSpec((1, ts, D), lambda b, s: (b, s, 0)),
            full((1, D)),
            full(w_in.shape),
            full((CONV_W, d_lru)),
            full((1, d_lru)),
            full(wg.shape),
            full(bg.shape),
            full((1, d_lru)),
            pl.BlockSpec((ts, hd), lambda b, s: (s, 0)),
            pl.BlockSpec((ts, hd), lambda b, s: (s, 0)),
            full(dmask.shape),
            full(qdec.shape),
            full(kdec.shape),
            full(cdec.shape),
            full(w_out.shape),
        ],
        out_specs=pl.BlockSpec((1, ts, D), lambda b, s: (b, s, 0)),
        out_shape=jax.ShapeDtypeStruct((B, S, D), F32),
        scratch_shapes=[
            pltpu.VMEM((SUBLANES, d_lru), F32),
            pltpu.VMEM((SUBLANES, d_lru), F32),
            pltpu.VMEM((SUBLANES, d_lru), F32),
            pltpu.VMEM((RET_HEADS, hd, hd), F32),
            pltpu.VMEM((ts, d_lru + d_ret), BF16),
        ],
        compiler_params=pltpu.CompilerParams(
            dimension_semantics=("arbitrary", "arbitrary"), vmem_limit_bytes=VMEM_LIMIT),
    )(x, g_mix.reshape(1, D), w_in.astype(BF16), conv_w, conv_b.reshape(1, d_lru), wg, bg,
      lam.reshape(1, d_lru), cos2, sin2, dmask, qdec, kdec, cdec, w_out.astype(BF16))


def _route_kernel(x_ref, g_ref, wr_ref, br_ref, hft_ref, info_ref, cnt_ref, carry):
    tm, D = x_ref.shape

    @pl.when(pl.program_id(0) == 0)
    def _():
        carry[...] = jnp.zeros_like(carry)

    hf = _rms(x_ref[...], g_ref[...])
    _store_token_tiles(hft_ref, hf)

    h_hi = hf.astype(BF16)
    h_lo = (hf - h_hi.astype(F32)).astype(BF16)
    logits = (jnp.dot(h_hi, wr_ref[0], preferred_element_type=F32)
              + (jnp.dot(h_hi, wr_ref[1], preferred_element_type=F32)
                 + jnp.dot(h_lo, wr_ref[0], preferred_element_type=F32))) + br_ref[...]
    lane = lax.broadcasted_iota(I32, logits.shape, 1)
    neg = jnp.float32(-jnp.inf)
    gmask = lane < N_GROUPS
    gl = jnp.where(gmask, logits, neg)
    gmax = jnp.max(gl, axis=-1, keepdims=True)
    g_sel = jnp.min(jnp.where(gl == gmax, lane, LANES), axis=-1, keepdims=True)
    g_w = 1.0 / jnp.sum(jnp.where(gmask, jnp.exp(logits - gmax), 0.0), axis=-1, keepdims=True)

    e_lane = lane - N_GROUPS
    emask = (e_lane >= 0) & (e_lane < N_EXPERTS) & ((e_lane >> 3) == g_sel)
    el = jnp.where(emask, logits, neg)
    v1 = jnp.max(el, axis=-1, keepdims=True)
    i1 = jnp.min(jnp.where(el == v1, e_lane, LANES), axis=-1, keepdims=True)
    el2 = jnp.where(e_lane == i1, neg, el)
    v2 = jnp.max(el2, axis=-1, keepdims=True)
    i2 = jnp.min(jnp.where(el2 == v2, e_lane, LANES), axis=-1, keepdims=True)
    t = jnp.exp(v2 - v1)
    w1 = g_w / (1.0 + t)
    w2 = g_w * (t / (1.0 + t))

    hit = (lane == i1) | (lane == i2)
    onehot = jnp.where(hit, 1.0, 0.0)
    rr = lax.broadcasted_iota(I32, (tm, tm), 0)
    cc = lax.broadcasted_iota(I32, (tm, tm), 1)
    tri = jnp.where(rr > cc, 1.0, 0.0).astype(BF16)
    prefix = jnp.dot(tri, onehot.astype(BF16), preferred_element_type=F32) + carry[0:1, :]
    r1 = jnp.sum(jnp.where(lane == i1, prefix, 0.0), axis=-1, keepdims=True)
    r2 = jnp.sum(jnp.where(lane == i2, prefix, 0.0), axis=-1, keepdims=True)
    carry[0:1, :] = carry[0:1, :] + jnp.sum(onehot, axis=0, keepdims=True)

    info = jnp.where(lane == 0, i1.astype(F32), 0.0)
    info = jnp.where(lane == 1, i2.astype(F32), info)
    info = jnp.where(lane == 2, w1, info)
    info = jnp.where(lane == 3, w2, info)
    info = jnp.where(lane == 4, r1, info)
    info = jnp.where(lane == 5, r2, info)
    info_ref[...] = info
    cnt_ref[...] = carry[...]


def _route(x1, g_ffn, w_grp, b_grp, w_exp, b_exp):
    T, D = x1.shape
    tm = ROUTE_TILE
    assert T % tm == 0 and N_GROUPS + N_EXPERTS <= LANES and T < (1 << 24) and D == 2 * TOKEN_ROWS * LANES
    pad = LANES - N_GROUPS - N_EXPERTS
    wr = jnp.concatenate([w_grp, w_exp, jnp.zeros((D, pad), F32)], axis=1)
    wr_hi = wr.astype(BF16)
    wr = jnp.stack([wr_hi, (wr - wr_hi.astype(F32)).astype(BF16)])
    br =jnp.concatenate([b_grp, b_exp, jnp.zeros((pad,), F32)]).reshape(1, LANES)
    return pl.pallas_call(
        _route_kernel,
        name="route",
        grid=(T // tm,),
        in_specs=[
            pl.BlockSpec((tm, D), lambda i: (i, 0)),
            pl.BlockSpec((1, D), lambda i: (0, 0)),
            pl.BlockSpec((2, D, LANES), lambda i: (0, 0, 0)),
            pl.BlockSpec((1, LANES), lambda i: (0, 0)),
        ],
        out_specs=[
            pl.BlockSpec((tm * TOKEN_ROWS, LANES), lambda i: (i, 0)),
            pl.BlockSpec((tm, LANES), lambda i: (i, 0)),
            pl.BlockSpec((SUBLANES, LANES), lambda i: (0, 0)),
        ],
        out_shape=[
            jax.ShapeDtypeStruct((T * TOKEN_ROWS, LANES), U32),
            jax.ShapeDtypeStruct((T, LANES), F32),
            jax.ShapeDtypeStruct((SUBLANES, LANES), F32),
        ],
        scratch_shapes=[pltpu.VMEM((SUBLANES, LANES), F32)],
        compiler_params=pltpu.CompilerParams(
            dimension_semantics=("arbitrary",), vmem_limit_bytes=VMEM_LIMIT),
    )(x1, g_ffn.reshape(1, D), wr, br)


def _dispatch_kernel(dest_ref, hft_ref, xs_in_ref, xs_ref, sem):
    del xs_in_ref
    tm = hft_ref.shape[0] // TOKEN_ROWS

    def start(t, c):
        _tile_copy(hft_ref, t * TOKEN_ROWS, xs_ref, dest_ref[2 * t], sem).start(priority=0)
        _tile_copy(hft_ref, t * TOKEN_ROWS, xs_ref, dest_ref[2 * t + 1], sem).start(priority=1)
        return c

    lax.fori_loop(0, tm, start, 0, unroll=DMA_UNROLL)

    for _ in range(2 * tm):
        _tile_copy(hft_ref, 0, xs_ref, 0, sem).wait()


def _dispatch(dest, hft, n_slots):
    T = hft.shape[0] // TOKEN_ROWS
    tm = DISPATCH_TILE
    assert T % tm == 0
    xs0 = jnp.zeros((n_slots * TOKEN_ROWS, LANES), U32)
    return pl.pallas_call(
        _dispatch_kernel,
        name="dispatch",
        grid=(T // tm,),
        in_specs=[
            pl.BlockSpec((2 * tm,), lambda i: (i,), memory_space=pltpu.SMEM),
            pl.BlockSpec((tm * TOKEN_ROWS, LANES), lambda i: (i, 0)),
            pl.BlockSpec(memory_space=pl.ANY),
        ],
        out_specs=pl.BlockSpec(memory_space=pl.ANY),
        out_shape=jax.ShapeDtypeStruct(xs0.shape, U32),
        scratch_shapes=[pltpu.SemaphoreType.DMA],
        input_output_aliases={2: 0},
        compiler_params=pltpu.CompilerParams(
            dimension_semantics=("arbitrary",), has_side_effects=True, vmem_limit_bytes=VMEM_LIMIT),
    )(dest, hft, xs0)


def _experts_kernel(first_ref, count_ref, xs_in_ref, w1_ref, w3_ref, w2_ref, ys_ref,
                    w1b, w3b, w2b, xbuf, ybuf, xb, sem_in, sem_out):
    del xs_in_ref
    e = pl.program_id(0)
    bm = xb.shape[0]
    rows = bm * TOKEN_ROWS
    first = first_ref[e]
    count = count_ref[e]

    w1b[...] = w1_ref[0].astype(BF16)
    w3b[...] = w3_ref[0].astype(BF16)
    w2b[...] = w2_ref[0].astype(BF16)

    def block_rows(j):
        return pl.ds(pl.multiple_of((first + j) * rows, rows), rows)

    def in_copy(j, s):
        return pltpu.make_async_copy(ys_ref.at[block_rows(j), :], xbuf.at[s], sem_in.at[s])

    def out_copy(j, s):
        return pltpu.make_async_copy(ybuf.at[s], ys_ref.at[block_rows(j), :], sem_out.at[s])

    @pl.when(count > 0)
    def _():
        in_copy(0, 0).start()

    def body(j, carry):
        s = lax.rem(j, 2)
        in_copy(j, s).wait()

        @pl.when(j + 1 < count)
        def _():
            in_copy(j + 1, 1 - s).start()

        @pl.when(j >= 2)
        def _():
            out_copy(j - 2, s).wait()

        half = TOKEN_ROWS * LANES
        for r in range(TOKEN_ROWS):
            hi, lo = _load_token_pair(xbuf.at[s], r, bm)
            xb[:, r * LANES:(r + 1) * LANES] = hi.astype(BF16)
            xb[:, half + r * LANES:half + (r + 1) * LANES] = lo.astype(BF16)
        x = xb[...]
        a1 = jnp.dot(x, w1b[...], preferred_element_type=F32)
        a3 = jnp.dot(x, w3b[...], preferred_element_type=F32)
        hmid = ((a1 * _sigmoid(a1)) * a3).astype(BF16)
        _store_token_tiles(ybuf.at[s], jnp.dot(hmid, w2b[...], preferred_element_type=F32))
        out_copy(j, s).start()
        return carry

    lax.fori_loop(0, count, body, 0)

    @pl.when(count >= 2)
    def _():
        out_copy(count - 2, lax.rem(count, 2)).wait()

    @pl.when(count >= 1)
    def _():
        out_copy(count - 1, lax.rem(count + 1, 2)).wait()


def _experts(first_block, block_count, xs, w1, w3, w2):
    bm = EXPERT_BLOCK
    E, D, DE = w1.shape
    assert D == 2 * TOKEN_ROWS * LANES
    rows = bm * TOKEN_ROWS
    grid_spec = pltpu.PrefetchScalarGridSpec(
        num_scalar_prefetch=2,
        grid=(E,),
        in_specs=[
            pl.BlockSpec(memory_space=pl.ANY),
            pl.BlockSpec((1, D, DE), lambda e, fb, bc: (e, 0, 0)),
            pl.BlockSpec((1, D, DE), lambda e, fb, bc: (e, 0, 0)),
            pl.BlockSpec((1, DE, D), lambda e, fb, bc: (e, 0, 0)),
        ],
        out_specs=pl.BlockSpec(memory_space=pl.ANY),
        scratch_shapes=[
            pltpu.VMEM((D, DE), BF16),
            pltpu.VMEM((D, DE), BF16),
            pltpu.VMEM((DE, D), BF16),
            pltpu.VMEM((2, rows, LANES), U32),
            pltpu.VMEM((2, rows, LANES), U32),
            pltpu.VMEM((bm, D), BF16),
            pltpu.SemaphoreType.DMA((2,)),
            pltpu.SemaphoreType.DMA((2,)),
        ],
    )
    return pl.pallas_call(
        _experts_kernel,
        name="experts",
        grid_spec=grid_spec,
        out_shape=jax.ShapeDtypeStruct(xs.shape, U32),
        input_output_aliases={2: 0},
        compiler_params=pltpu.CompilerParams(
            dimension_semantics=("arbitrary",), has_side_effects=True, vmem_limit_bytes=VMEM_LIMIT),
    )(first_block, block_count, xs, w1, w3, w2)


def _combine_kernel(dest0_ref, dest1_ref, dest2_ref, x1_ref, p_ref, info_ref, ys_ref, gple_ref, wgate_ref, bgate_ref,
                    wproj_ref, gfin_ref, o_ref, gbuf, sems, *, final_norm):
    tm = x1_ref.shape[0]
    i = pl.program_id(0)
    last = pl.num_programs(0) - 1
    slot = lax.rem(i, GATHER_SLOTS)
    ahead = lax.rem(i + GATHER_SLOTS - 1, GATHER_SLOTS)

    def start_pair(idx_ref, t, s):
        _tile_copy(ys_ref, idx_ref[2 * t], gbuf.at[s, 0], t * TOKEN_ROWS, sems.at[s]).start(priority=0)
        _tile_copy(ys_ref, idx_ref[2 * t + 1], gbuf.at[s, 1], t * TOKEN_ROWS, sems.at[s]).start(priority=1)

    def wait_all(s):
        for _ in range(2 * tm):
            _tile_copy(ys_ref, 0, gbuf.at[s, 0], 0, sems.at[s]).wait()

    @pl.when(i == 0)
    def _():
        def body(t, c):
            start_pair(dest0_ref, t, 0)
            start_pair(dest1_ref, t, 1)
            return c
        lax.fori_loop(0, tm, body, 0, unroll=DMA_UNROLL)

    wait_all(slot)
    info = info_ref[...]
    w_a = info[:, 2:3]
    w_b = info[:, 3:4]
    his, los = [], []
    for j in range(TOKEN_ROWS):
        a_hi, a_lo = _load_token_pair(gbuf.at[slot, 0], j, tm)
        b_hi, b_lo = _load_token_pair(gbuf.at[slot, 1], j, tm)
        his.append(a_hi * w_a + b_hi * w_b)
        los.append(a_lo * w_a + b_lo * w_b)
    x2 = x1_ref[...] + jnp.concatenate(his + los, axis=1)

    for t in range(tm):
        start_pair(dest2_ref, t, ahead)

    pp = jnp.dot(p_ref[...].astype(BF16), wproj_ref[...], preferred_element_type=F32)
    gate = _sigmoid(jnp.dot(_rms(x2, gple_ref[...]).astype(BF16), wgate_ref[...], preferred_element_type=F32)
                    + bgate_ref[...])
    x3 = x2 + gate * pp
    o_ref[...] = _rms(x3, gfin_ref[...]) if final_norm else x3

    @pl.when(i == last)
    def _():
        wait_all(lax.rem(i + 1, GATHER_SLOTS))
        wait_all(ahead)


def _combine(dest, x1, p, info, ys, g_ple, w_gate, b_gate, w_proj, g_final, final_norm):
    T, D = x1.shape
    PD = p.shape[1]
    tm = COMBINE_TILE
    assert T % tm == 0 and D == 2 * TOKEN_ROWS * LANES
    n_steps = T // tm
    const = lambda shape: pl.BlockSpec(shape, lambda i: (0,) * len(shape))
    return pl.pallas_call(
        functools.partial(_combine_kernel, final_norm=final_norm),
        name="combine",
        grid=(n_steps,),
        in_specs=[
            pl.BlockSpec((2 * tm,), lambda i: (i,), memory_space=pltpu.SMEM),
            pl.BlockSpec((2 * tm,), lambda i: (jnp.minimum(i + 1, n_steps - 1),), memory_space=pltpu.SMEM),
            pl.BlockSpec((2 * tm,), lambda i: (jnp.minimum(i + 2, n_steps - 1),), memory_space=pltpu.SMEM),
            pl.BlockSpec((tm, D), lambda i: (i, 0)),
            pl.BlockSpec((tm, PD), lambda i: (i, 0)),
            pl.BlockSpec((tm, LANES), lambda i: (i, 0)),
            pl.BlockSpec(memory_space=pl.ANY),
            const((1, D)),
            const((D, D)),
            const((1, D)),
            const((PD, D)),
            const((1, D)),
        ],
        out_specs=pl.BlockSpec((tm, D), lambda i: (i, 0)),
        out_shape=jax.ShapeDtypeStruct((T, D), F32),
        scratch_shapes=[pltpu.VMEM((GATHER_SLOTS, 2, tm * TOKEN_ROWS, LANES), U32),
                        pltpu.SemaphoreType.DMA((GATHER_SLOTS,))],
        compiler_params=pltpu.CompilerParams(
            dimension_semantics=("arbitrary",), vmem_limit_bytes=VMEM_LIMIT),
    )(dest, dest, dest, x1, p, info, ys, g_ple.reshape(1, D), w_gate.astype(BF16), b_gate.reshape(1, D),
      w_proj.astype(BF16), g_final.reshape(1, D))


def _slot_plan(info, counts_f, n_assign):
    bm = EXPERT_BLOCK
    counts = counts_f[0, :N_EXPERTS].astype(I32)
    padded = (counts + bm - 1) // bm * bm
    pad_end = jnp.cumsum(padded)
    pad_start = pad_end - padded
    n_blocks = -(-n_assign // bm) + N_EXPERTS
    e = info[:, 0:2].astype(I32)
    rank = info[:, 4:6].astype(I32)
    hit = e[:, :, None] == jnp.arange(N_EXPERTS, dtype=I32)[None, None, :]
    dest = (jnp.sum(jnp.where(hit, pad_start[None, None, :], 0), axis=-1) + rank).reshape(-1)
    return dest * TOKEN_ROWS, pad_start // bm, padded // bm, n_blocks * bm


def kernel(x, p, g_mix, w_in, conv_w, conv_b, lru_wa, lru_ba, lru_wx, lru_bx, lru_lambda, w_out, g_ffn, w_router_group, b_router_group, w_router_expert, b_router_expert, w1, w3, w2, g_ple, w_ple_gate, b_ple_gate, w_ple_proj, g_final):
    B, S, D = x.shape
    T = B * S
    depth = g_mix.shape[0]
    for l in range(depth):
        x1 = _mix(x, g_mix[l], w_in[l], conv_w[l], conv_b[l], lru_wa[l], lru_ba[l], lru_wx[l], lru_bx[l],
                  lru_lambda[l], w_out[l]).reshape(T, D)
        hft, info, counts = _route(x1, g_ffn[l], w_router_group[l], b_router_group[l],
                                   w_router_expert[l], b_router_expert[l])
        dest, first_block, block_count, n_slots = _slot_plan(info, counts, 2 * T)
        xs = _dispatch(dest, hft, n_slots)
        ys = _experts(first_block, block_count, xs, w1[l], w3[l], w2[l])
        x = _combine(dest, x1, p[l].reshape(T, -1), info, ys, g_ple[l], w_ple_gate[l], b_ple_gate[l],
                     w_ple_proj[l], g_final, l == depth - 1).reshape(B, S, D)
    return x
```

```python
import functools

import jax
import jax.numpy as jnp
from jax import lax
from jax.experimental import pallas as pl
from jax.experimental.pallas import tpu as pltpu

F32 = jnp.float32
BF16 = jnp.bfloat16
I32 = jnp.int32
U32 = jnp.uint32

EPS = 1e-6
LRU_C = 8.0
CONV_W = 4
RET_HEADS = 4
ROPE_BASE = 10000.0
N_GROUPS = 8
EXPERTS_PER_GROUP = 8
N_EXPERTS = N_GROUPS * EXPERTS_PER_GROUP

LANES = 128
SUBLANES = 8
VMEM_LIMIT = 56 * 1024 * 1024

MIX_TILE = 512
RET_CHUNK = 256
ROUTE_TILE = 512
DISPATCH_TILE = 512
EXPERT_BLOCK = 256
COMBINE_TILE = 256
DMA_UNROLL = 8
GATHER_SLOTS = 3
IN_RING = 3
OUT_RING = 2


def _rms(x, g):
    ms = jnp.mean(x * x, axis=-1, keepdims=True)
    return (x * lax.rsqrt(ms + EPS)) * g


def _sigmoid(z):
    return 1.0 / (1.0 + jnp.exp(-z))


TOKEN_ROWS = 4


def _pack_pair(a, b):
    ab = lax.bitcast_convert_type(a.astype(BF16).astype(F32), U32)
    bb = lax.bitcast_convert_type(b.astype(BF16).astype(F32), U32)
    return ab | (bb >> 16)


def _unpack_pair(w):
    hi = lax.bitcast_convert_type(w & jnp.uint32(0xFFFF0000), F32)
    lo = lax.bitcast_convert_type(w << 16, F32)
    return hi, lo


def _store_token_tiles(tile_ref, x):
    n = x.shape[0]
    half = TOKEN_ROWS * LANES
    for j in range(TOKEN_ROWS):
        tile_ref[pl.ds(j, n, stride=TOKEN_ROWS), :] = _pack_pair(x[:, j * LANES:(j + 1) * LANES],
                                                                 x[:, half + j * LANES:half + (j + 1) * LANES])


def _load_token_pair(tile_ref, j, n):
    return _unpack_pair(tile_ref[pl.ds(j, n, stride=TOKEN_ROWS), :])


def _tile_copy(src_ref, src_row, dst_ref, dst_row, sem):
    return pltpu.make_async_copy(src_ref.at[pl.ds(pl.multiple_of(src_row, TOKEN_ROWS), TOKEN_ROWS), :],
                                 dst_ref.at[pl.ds(pl.multiple_of(dst_row, TOKEN_ROWS), TOKEN_ROWS), :], sem)


def _group_view(x):
    return x.reshape(x.shape[0] // SUBLANES, SUBLANES, x.shape[1])


def _shift_rows(x3, d, prev_group):
    r = pltpu.roll(x3, d, 1)
    prev = jnp.concatenate([pltpu.roll(prev_group[None], d, 1), r[:-1]], axis=0)
    sub = lax.broadcasted_iota(I32, x3.shape, 1)
    return jnp.where(sub < d, prev, r)


def _linear_scan(a, u, h0):
    a3, u3 = _group_view(a), _group_view(u)
    sub = lax.broadcasted_iota(I32, a3.shape, 1)
    d = 1
    while d < SUBLANES:
        keep = sub >= d
        a_sh = jnp.where(keep, pltpu.roll(a3, d, 1), 1.0)
        u_sh = jnp.where(keep, pltpu.roll(u3, d, 1), 0.0)
        u3 = u3 + a3 * u_sh
        a3 = a3 * a_sh
        d *= 2
    carry = h0
    groups = []
    for g in range(a3.shape[0]):
        hg = u3[g] + a3[g] * carry
        groups.append(hg)
        carry = hg[SUBLANES - 1:SUBLANES, :]
    return jnp.concatenate(groups, axis=0)


def _mix_kernel(x_ref, gmix_ref, win_ref, convw_ref, convb_ref, wg_ref, bg_ref, lam_ref,
                cos_ref, sin_ref, dmask_ref, qdec_ref, kdec_ref, cdec_ref, wout_ref,
                o_ref, xtail, qtail, hcar, state, ybuf):
    ts = x_ref.shape[1]
    d_lru = lam_ref.shape[1]
    d_ret = qdec_ref.shape[1]
    hd = d_ret // RET_HEADS
    chunk = dmask_ref.shape[1]
    half = d_lru // 2

    @pl.when(pl.program_id(1) == 0)
    def _():
        xtail[...] = jnp.zeros_like(xtail)
        qtail[...] = jnp.zeros_like(qtail)
        hcar[...] = jnp.zeros_like(hcar)
        state[...] = jnp.zeros_like(state)

    x = x_ref[0]
    h = _rms(x, gmix_ref[...]).astype(BF16)

    def proj(lo, width):
        return jnp.dot(h, win_ref[:, lo:lo + width], preferred_element_type=F32)

    xl = proj(0, d_lru)

    assert CONV_W == 4
    x3 = _group_view(xl)
    xm1 = _shift_rows(x3, 1, xtail[...])
    q3 = xm1 * convw_ref[0:1, :] + x3 * convw_ref[1:2, :]
    p3 = xm1 * convw_ref[2:3, :] + x3 * convw_ref[3:4, :]
    xc = ((convb_ref[...] + _shift_rows(q3, 2, qtail[...])) + p3).reshape(ts, d_lru)
    xtail[...] = x3[-1]
    qtail[...] = q3[-1]

    xcb = xc.astype(BF16)
    g0 = jnp.dot(xcb[:, :half], wg_ref[0], preferred_element_type=F32)
    g1 = jnp.dot(xcb[:, half:], wg_ref[1], preferred_element_type=F32)
    ra = jnp.concatenate([g0[:, :half], g1[:, :half]], axis=1) + bg_ref[:, :d_lru]
    ix = jnp.concatenate([g0[:, half:], g1[:, half:]], axis=1) + bg_ref[:, d_lru:]
    r = _sigmoid(ra)
    i_gate = _sigmoid(ix)
    z = -lam_ref[...]
    softplus = jnp.maximum(z, 0.0) + jnp.log1p(jnp.exp(-jnp.abs(z)))
    log_a = (-LRU_C * r) * softplus
    a = jnp.exp(log_a)
    v = 1.0 - a * a
    u = jnp.where(v > 0.0, v * lax.rsqrt(v), 0.0) * (i_gate * xc)
    hseq = _linear_scan(a, u, hcar[0:1, :])
    hcar[0:1, :] = hseq[ts - 1:ts, :]
    gl = proj(d_lru, d_lru)
    ybuf[:, 0:d_lru] = (hseq * jax.nn.gelu(gl)).astype(BF16)

    base = 2 * d_lru
    q = proj(base, d_ret)
    k = proj(base + d_ret, d_ret)
    v = proj(base + 2 * d_ret, d_ret)
    gr = proj(base + 3 * d_ret, d_ret)
    scale = hd ** -0.5
    for hh in range(RET_HEADS):
        ls = slice(hh * hd, (hh + 1) * hd)
        for c in range(ts // chunk):
            rs = slice(c * chunk, (c + 1) * chunk)
            cosv = cos_ref[rs, :]
            sinv = sin_ref[rs, :]
            qh = q[rs, ls]
            kh = k[rs, ls]
            qr = qh * cosv + pltpu.roll(qh, hd // 2, 1) * sinv
            kr = (kh * cosv + pltpu.roll(kh, hd // 2, 1) * sinv) * scale
            vb = v[rs, ls].astype(BF16)
            scores = lax.dot_general(qr.astype(BF16), kr.astype(BF16), (((1,), (1,)), ((), ())),
                                     preferred_element_type=F32)
            inner = jnp.dot((scores * dmask_ref[hh]).astype(BF16), vb, preferred_element_type=F32)
            st = state[hh]
            cross = jnp.dot((qr * qdec_ref[:, ls]).astype(BF16), st.astype(BF16), preferred_element_type=F32)
            kv = lax.dot_general((kr * kdec_ref[:, ls]).astype(BF16), vb, (((0,), (0,)), ((), ())),
                                 preferred_element_type=F32)
            state[hh] = st * cdec_ref[:, ls] + kv
            o = inner + cross
            mu = jnp.mean(o, axis=-1, keepdims=True)
            oc = o - mu
            var = jnp.mean(oc * oc, axis=-1, keepdims=True)
            on = oc * lax.rsqrt(var + EPS)
            grh = gr[rs, ls]
            ybuf[rs, d_lru + hh * hd:d_lru + (hh + 1) * hd] = ((grh * _sigmoid(grh)) * on).astype(BF16)

    o_ref[0] = x + jnp.dot(ybuf[...], wout_ref[...], preferred_element_type=F32)


def _mix(x, g_mix, w_in, conv_w, conv_b, lru_wa, lru_ba, lru_wx, lru_bx, lam, w_out):
    B, S, D = x.shape
    d_lru = lam.shape[0]
    d_ret = (w_in.shape[1] - 2 * d_lru) // 4
    hd = d_ret // RET_HEADS
    ts, chunk = MIX_TILE, RET_CHUNK
    assert S % ts == 0 and ts % chunk == 0 and d_lru % (2 * LANES) == 0 and hd == LANES
    nb, bw = lru_wa.shape[0], lru_wa.shape[1]
    half = d_lru // 2
    per_half = nb // 2

    def blockdiag(w):
        out = jnp.zeros((half, half), F32)
        for j in range(per_half):
            out = out.at[j * bw:(j + 1) * bw, j * bw:(j + 1) * bw].set(w[j])
        return out
    wg = jnp.stack([jnp.concatenate([blockdiag(lru_wa[hf * per_half:(hf + 1) * per_half]),
                                     blockdiag(lru_wx[hf * per_half:(hf + 1) * per_half])], axis=1)
                    for hf in range(2)]).astype(BF16)
    bg = jnp.concatenate([lru_ba.reshape(1, d_lru), lru_bx.reshape(1, d_lru)], axis=1)

    inv = ROPE_BASE ** (-jnp.arange(hd // 2, dtype=F32) / (hd // 2))
    ang = jnp.arange(S, dtype=F32)[:, None] * inv[None, :]
    cos2 = jnp.concatenate([jnp.cos(ang), jnp.cos(ang)], axis=1)
    sin2 = jnp.concatenate([-jnp.sin(ang), jnp.sin(ang)], axis=1)
    log_g = jnp.log(1.0 - 2.0 ** (-5.0 - jnp.arange(RET_HEADS, dtype=F32)))
    idx = jnp.arange(chunk, dtype=F32)
    diff = idx[:, None] - idx[None, :]
    dmask = jnp.where(diff >= 0, jnp.exp(jnp.maximum(diff, 0.0)[None] * log_g[:, None, None]), 0.0)
    rep = lambda t: jnp.repeat(t, hd, axis=-1)
    qdec = rep(jnp.exp((idx + 1.0)[:, None] * log_g[None, :]))
    kdec = rep(jnp.exp((chunk - 1.0 - idx)[:, None] * log_g[None, :]))
    cdec = rep(jnp.exp(chunk * log_g)[None, :])

    full = lambda shape: pl.BlockSpec(shape, lambda b, s: (0,) * len(shape))
    return pl.pallas_call(
        _mix_kernel,
        name="mix",
        grid=(B, S // ts),
        in_specs=[
            pl.BlockSpec((1, ts, D), lambda b, s: (b, s, 0)),
            full((1, D)),
            full(w_in.shape),
            full((CONV_W, d_lru)),
            full((1, d_lru)),
            full(wg.shape),
            full(bg.shape),
            full((1, d_lru)),
            pl.BlockSpec((ts, hd), lambda b, s: (s, 0)),
            pl.BlockSpec((ts, hd), lambda b, s: (s, 0)),
            full(dmask.shape),
            full(qdec.shape),
            full(kdec.shape),
            full(cdec.shape),
            full(w_out.shape),
        ],
        out_specs=pl.BlockSpec((1, ts, D), lambda b, s: (b, s, 0)),
        out_shape=jax.ShapeDtypeStruct((B, S, D), F32),
        scratch_shapes=[
            pltpu.VMEM((SUBLANES, d_lru), F32),
            pltpu.VMEM((SUBLANES, d_lru), F32),
            pltpu.VMEM((SUBLANES, d_lru), F32),
            pltpu.VMEM((RET_HEADS, hd, hd), F32),
            pltpu.VMEM((ts, d_lru + d_ret), BF16),
        ],
        compiler_params=pltpu.CompilerParams(
            dimension_semantics=("arbitrary", "arbitrary"), vmem_limit_bytes=VMEM_LIMIT),
    )(x, g_mix.reshape(1, D), w_in.astype(BF16), conv_w, conv_b.reshape(1, d_lru), wg, bg,
      lam.reshape(1, d_lru), cos2, sin2, dmask, qdec, kdec, cdec, w_out.astype(BF16))


def _route_kernel(x_ref, g_ref, wr_ref, br_ref, hft_ref, info_ref, cnt_ref, carry):
    tm, D = x_ref.shape

    @pl.when(pl.program_id(0) == 0)
    def _():
        carry[...] = jnp.zeros_like(carry)

    hf = _rms(x_ref[...], g_ref[...])
    _store_token_tiles(hft_ref, hf)

    h_hi = hf.astype(BF16)
    h_lo = (hf - h_hi.astype(F32)).astype(BF16)
    logits = (jnp.dot(h_hi, wr_ref[0], preferred_element_type=F32)
              + (jnp.dot(h_hi, wr_ref[1], preferred_element_type=F32)
                 + jnp.dot(h_lo, wr_ref[0], preferred_element_type=F32))) + br_ref[...]
    lane = lax.broadcasted_iota(I32, logits.shape, 1)
    neg = jnp.float32(-jnp.inf)
    gmask = lane < N_GROUPS
    gl = jnp.where(gmask, logits, neg)
    gmax = jnp.max(gl, axis=-1, keepdims=True)
    g_sel = jnp.min(jnp.where(gl == gmax, lane, LANES), axis=-1, keepdims=True)
    g_w = 1.0 / jnp.sum(jnp.where(gmask, jnp.exp(logits - gmax), 0.0), axis=-1, keepdims=True)

    e_lane = lane - N_GROUPS
    emask = (e_lane >= 0) & (e_lane < N_EXPERTS) & ((e_lane >> 3) == g_sel)
    el = jnp.where(emask, logits, neg)
    v1 = jnp.max(el, axis=-1, keepdims=True)
    i1 = jnp.min(jnp.where(el == v1, e_lane, LANES), axis=-1, keepdims=True)
    el2 = jnp.where(e_lane == i1, neg, el)
    v2 = jnp.max(el2, axis=-1, keepdims=True)
    i2 = jnp.min(jnp.where(el2 == v2, e_lane, LANES), axis=-1, keepdims=True)
    t = jnp.exp(v2 - v1)
    w1 = g_w / (1.0 + t)
    w2 = g_w * (t / (1.0 + t))

    hit = (lane == i1) | (lane == i2)
    onehot = jnp.where(hit, 1.0, 0.0)
    rr = lax.broadcasted_iota(I32, (tm, tm), 0)
    cc = lax.broadcasted_iota(I32, (tm, tm), 1)
    tri = jnp.where(rr > cc, 1.0, 0.0).astype(BF16)
    prefix = jnp.dot(tri, onehot.astype(BF16), preferred_element_type=F32) + carry[0:1, :]
    r1 = jnp.sum(jnp.where(lane == i1, prefix, 0.0), axis=-1, keepdims=True)
    r2 = jnp.sum(jnp.where(lane == i2, prefix, 0.0), axis=-1, keepdims=True)
    carry[0:1, :] = carry[0:1, :] + jnp.sum(onehot, axis=0, keepdims=True)

    info = jnp.where(lane == 0, i1.astype(F32), 0.0)
    info = jnp.where(lane == 1, i2.astype(F32), info)
    info = jnp.where(lane == 2, w1, info)
    info = jnp.where(lane == 3, w2, info)
    info = jnp.where(lane == 4, r1, info)
    info = jnp.where(lane == 5, r2, info)
    info_ref[...] = info
    cnt_ref[...] = carry[...]


def _route(x1, g_ffn, w_grp, b_grp, w_exp, b_exp):
    T, D = x1.shape
    tm = ROUTE_TILE
    assert T % tm == 0 and N_GROUPS + N_EXPERTS <= LANES and T < (1 << 24) and D == 2 * TOKEN_ROWS * LANES
    pad = LANES - N_GROUPS - N_EXPERTS
    wr = jnp.concatenate([w_grp, w_exp, jnp.zeros((D, pad), F32)], axis=1)
    wr_hi = wr.astype(BF16)
    wr = jnp.stack([wr_hi, (wr - wr_hi.astype(F32)).astype(BF16)])
    br =jnp.concatenate([b_grp, b_exp, jnp.zeros((pad,), F32)]).reshape(1, LANES)
    return pl.pallas_call(
        _route_kernel,
        name="route",
        grid=(T // tm,),
        in_specs=[
            pl.BlockSpec((tm, D), lambda i: (i, 0)),
            pl.BlockSpec((1, D), lambda i: (0, 0)),
            pl.BlockSpec((2, D, LANES), lambda i: (0, 0, 0)),
            pl.BlockSpec((1, LANES), lambda i: (0, 0)),
        ],
        out_specs=[
            pl.BlockSpec((tm * TOKEN_ROWS, LANES), lambda i: (i, 0)),
            pl.BlockSpec((tm, LANES), lambda i: (i, 0)),
            pl.BlockSpec((SUBLANES, LANES), lambda i: (0, 0)),
        ],
        out_shape=[
            jax.ShapeDtypeStruct((T * TOKEN_ROWS, LANES), U32),
            jax.ShapeDtypeStruct((T, LANES), F32),
            jax.ShapeDtypeStruct((SUBLANES, LANES), F32),
        ],
        scratch_shapes=[pltpu.VMEM((SUBLANES, LANES), F32)],
        compiler_params=pltpu.CompilerParams(
            dimension_semantics=("arbitrary",), vmem_limit_bytes=VMEM_LIMIT),
    )(x1, g_ffn.reshape(1, D), wr, br)


def _dispatch_kernel(dest_ref, hft_ref, xs_in_ref, xs_ref, sem):
    del xs_in_ref
    tm = hft_ref.shape[0] // TOKEN_ROWS

    def start(t, c):
        _tile_copy(hft_ref, t * TOKEN_ROWS, xs_ref, dest_ref[2 * t], sem).start(priority=0)
        _tile_copy(hft_ref, t * TOKEN_ROWS, xs_ref, dest_ref[2 * t + 1], sem).start(priority=1)
        return c

    lax.fori_loop(0, tm, start, 0, unroll=DMA_UNROLL)

    for _ in range(2 * tm):
        _tile_copy(hft_ref, 0, xs_ref, 0, sem).wait()


def _dispatch(dest, hft, n_slots):
    T = hft.shape[0] // TOKEN_ROWS
    tm = DISPATCH_TILE
    assert T % tm == 0
    xs0 = jnp.zeros((n_slots * TOKEN_ROWS, LANES), U32)
    return pl.pallas_call(
        _dispatch_kernel,
        name="dispatch",
        grid=(T // tm,),
        in_specs=[
            pl.BlockSpec((2 * tm,), lambda i: (i,), memory_space=pltpu.SMEM),
            pl.BlockSpec((tm * TOKEN_ROWS, LANES), lambda i: (i, 0)),
            pl.BlockSpec(memory_space=pl.ANY),
        ],
        out_specs=pl.BlockSpec(memory_space=pl.ANY),
        out_shape=jax.ShapeDtypeStruct(xs0.shape, U32),
        scratch_shapes=[pltpu.SemaphoreType.DMA],
        input_output_aliases={2: 0},
        compiler_params=pltpu.CompilerParams(
            dimension_semantics=("arbitrary",), has_side_effects=True, vmem_limit_bytes=VMEM_LIMIT),
    )(dest, hft, xs0)


def _experts_kernel(first_ref, count_ref, xs_in_ref, w1_hbm, w3_hbm, w2_hbm, ys_ref,
                    w1f, w3f, w2f, w1b, w3b, w2b, xbuf, ybuf, xb, sem_w, sem_in, sem_out):
    del xs_in_ref
    e = pl.program_id(0)
    n_exp = pl.num_programs(0)
    bm = xb.shape[0]
    rows = bm * TOKEN_ROWS
    first = first_ref[e]
    count = count_ref[e]
    total = first_ref[n_exp - 1] + count_ref[n_exp - 1]
    wslot = lax.rem(e, 2)

    def weight_copies(ex, s):
        return [pltpu.make_async_copy(hbm.at[ex], buf.at[s], sem_w.at[s])
                for hbm, buf in ((w1_hbm, w1f), (w3_hbm, w3f), (w2_hbm, w2f))]

    def block_rows(g):
        return pl.ds(pl.multiple_of(g * rows, rows), rows)

    def in_copy(g):
        s = lax.rem(g, IN_RING)
        return pltpu.make_async_copy(ys_ref.at[block_rows(g), :], xbuf.at[s], sem_in.at[s])

    def out_copy(g):
        s = lax.rem(g, OUT_RING)
        return pltpu.make_async_copy(ybuf.at[s], ys_ref.at[block_rows(g), :], sem_out.at[s])

    @pl.when(e == 0)
    def _():
        for c in weight_copies(0, 0):
            c.start(priority=1)
        for k in range(IN_RING - 1):
            @pl.when(k < total)
            def _():
                in_copy(k).start()

    @pl.when(e + 1 < n_exp)
    def _():
        for c in weight_copies(e + 1, 1 - wslot):
            c.start(priority=1)

    for c in weight_copies(e, wslot):
        c.wait()
    w1b[...] = w1f[wslot].astype(BF16)
    w3b[...] = w3f[wslot].astype(BF16)
    w2b[...] = w2f[wslot].astype(BF16)

    def body(j, carry):
        g = first + j
        in_copy(g).wait()

        @pl.when(g + IN_RING - 1 < total)
        def _():
            in_copy(g + IN_RING - 1).start()

        @pl.when(g >= OUT_RING)
        def _():
            out_copy(g - OUT_RING).wait()

        s = lax.rem(g, IN_RING)
        half = TOKEN_ROWS * LANES
        for r in range(TOKEN_ROWS):
            hi, lo = _load_token_pair(xbuf.at[s], r, bm)
            xb[:, r * LANES:(r + 1) * LANES] = hi.astype(BF16)
            xb[:, half + r * LANES:half + (r + 1) * LANES] = lo.astype(BF16)
        x = xb[...]
        a1 = jnp.dot(x, w1b[...], preferred_element_type=F32)
        a3 = jnp.dot(x, w3b[...], preferred_element_type=F32)
        hmid = ((a1 * _sigmoid(a1)) * a3).astype(BF16)
        _store_token_tiles(ybuf.at[lax.rem(g, OUT_RING)], jnp.dot(hmid, w2b[...], preferred_element_type=F32))
        out_copy(g).start()
        return carry

    lax.fori_loop(0, count, body, 0)

    @pl.when(e == n_exp - 1)
    def _():
        for k in range(OUT_RING):
            @pl.when(total - 1 - k >= 0)
            def _():
                out_copy(total - 1 - k).wait()


def _experts(first_block, block_count, xs, w1, w3, w2):
    bm = EXPERT_BLOCK
    E, D, DE = w1.shape
    assert D == 2 * TOKEN_ROWS * LANES
    rows = bm * TOKEN_ROWS
    grid_spec = pltpu.PrefetchScalarGridSpec(
        num_scalar_prefetch=2,
        grid=(E,),
        in_specs=[
            pl.BlockSpec(memory_space=pl.ANY),
            pl.BlockSpec(memory_space=pl.ANY),
            pl.BlockSpec(memory_space=pl.ANY),
            pl.BlockSpec(memory_space=pl.ANY),
        ],
        out_specs=pl.BlockSpec(memory_space=pl.ANY),
        scratch_shapes=[
            pltpu.VMEM((2, D, DE), F32),
            pltpu.VMEM((2, D, DE), F32),
            pltpu.VMEM((2, DE, D), F32),
            pltpu.VMEM((D, DE), BF16),
            pltpu.VMEM((D, DE), BF16),
            pltpu.VMEM((DE, D), BF16),
            pltpu.VMEM((IN_RING, rows, LANES), U32),
            pltpu.VMEM((OUT_RING, rows, LANES), U32),
            pltpu.VMEM((bm, D), BF16),
            pltpu.SemaphoreType.DMA((2,)),
            pltpu.SemaphoreType.DMA((IN_RING,)),
            pltpu.SemaphoreType.DMA((OUT_RING,)),
        ],
    )
    return pl.pallas_call(
        _experts_kernel,
        name="experts",
        grid_spec=grid_spec,
        out_shape=jax.ShapeDtypeStruct(xs.shape, U32),
        input_output_aliases={2: 0},
        compiler_params=pltpu.CompilerParams(
            dimension_semantics=("arbitrary",), has_side_effects=True, vmem_limit_bytes=VMEM_LIMIT),
    )(first_block, block_count, xs, w1, w3, w2)


def _combine_kernel(dest0_ref, dest1_ref, dest2_ref, x1_ref, p_ref, info_ref, ys_ref, gple_ref, wgate_ref, bgate_ref,
                    wproj_ref, gfin_ref, o_ref, gbuf, sems, *, final_norm):
    tm = x1_ref.shape[0]
    i = pl.program_id(0)
    last = pl.num_programs(0) - 1
    slot = lax.rem(i, GATHER_SLOTS)
    ahead = lax.rem(i + GATHER_SLOTS - 1, GATHER_SLOTS)

    def start_pair(idx_ref, t, s):
        _tile_copy(ys_ref, idx_ref[2 * t], gbuf.at[s, 0], t * TOKEN_ROWS, sems.at[s]).start(priority=0)
        _tile_copy(ys_ref, idx_ref[2 * t + 1], gbuf.at[s, 1], t * TOKEN_ROWS, sems.at[s]).start(priority=1)

    def wait_all(s):
        for _ in range(2 * tm):
            _tile_copy(ys_ref, 0, gbuf.at[s, 0], 0, sems.at[s]).wait()

    @pl.when(i == 0)
    def _():
        def body(t, c):
            start_pair(dest0_ref, t, 0)
            start_pair(dest1_ref, t, 1)
            return c
        lax.fori_loop(0, tm, body, 0, unroll=DMA_UNROLL)

    wait_all(slot)
    info = info_ref[...]
    w_a = info[:, 2:3]
    w_b = info[:, 3:4]
    his, los = [], []
    for j in range(TOKEN_ROWS):
        a_hi, a_lo = _load_token_pair(gbuf.at[slot, 0], j, tm)
        b_hi, b_lo = _load_token_pair(gbuf.at[slot, 1], j, tm)
        his.append(a_hi * w_a + b_hi * w_b)
        los.append(a_lo * w_a + b_lo * w_b)
    x2 = x1_ref[...] + jnp.concatenate(his + los, axis=1)

    for t in range(tm):
        start_pair(dest2_ref, t, ahead)

    pp = jnp.dot(p_ref[...].astype(BF16), wproj_ref[...], preferred_element_type=F32)
    gate = _sigmoid(jnp.dot(_rms(x2, gple_ref[...]).astype(BF16), wgate_ref[...], preferred_element_type=F32)
                    + bgate_ref[...])
    x3 = x2 + gate * pp
    o_ref[...] = _rms(x3, gfin_ref[...]) if final_norm else x3

    @pl.when(i == last)
    def _():
        wait_all(lax.rem(i + 1, GATHER_SLOTS))
        wait_all(ahead)


def _combine(dest, x1, p, info, ys, g_ple, w_gate, b_gate, w_proj, g_final, final_norm):
    T, D = x1.shape
    PD = p.shape[1]
    tm = COMBINE_TILE
    assert T % tm == 0 and D == 2 * TOKEN_ROWS * LANES
    n_steps = T // tm
    const = lambda shape: pl.BlockSpec(shape, lambda i: (0,) * len(shape))
    return pl.pallas_call(
        functools.partial(_combine_kernel, final_norm=final_norm),
        name="combine",
        grid=(n_steps,),
        in_specs=[
            pl.BlockSpec((2 * tm,), lambda i: (i,), memory_space=pltpu.SMEM),
            pl.BlockSpec((2 * tm,), lambda i: (jnp.minimum(i + 1, n_steps - 1),), memory_space=pltpu.SMEM),
            pl.BlockSpec((2 * tm,), lambda i: (jnp.minimum(i + 2, n_steps - 1),), memory_space=pltpu.SMEM),
            pl.BlockSpec((tm, D), lambda i: (i, 0)),
            pl.BlockSpec((tm, PD), lambda i: (i, 0)),
            pl.BlockSpec((tm, LANES), lambda i: (i, 0)),
            pl.BlockSpec(memory_space=pl.ANY),
            const((1, D)),
            const((D, D)),
            const((1, D)),
            const((PD, D)),
            const((1, D)),
        ],
        out_specs=pl.BlockSpec((tm, D), lambda i: (i, 0)),
        out_shape=jax.ShapeDtypeStruct((T, D), F32),
        scratch_shapes=[pltpu.VMEM((GATHER_SLOTS, 2, tm * TOKEN_ROWS, LANES), U32),
                        pltpu.SemaphoreType.DMA((GATHER_SLOTS,))],
        compiler_params=pltpu.CompilerParams(
            dimension_semantics=("arbitrary",), vmem_limit_bytes=VMEM_LIMIT),
    )(dest, dest, dest, x1, p, info, ys, g_ple.reshape(1, D), w_gate.astype(BF16), b_gate.reshape(1, D),
      w_proj.astype(BF16), g_final.reshape(1, D))


def _slot_plan(info, counts_f, n_assign):
    bm = EXPERT_BLOCK
    counts = counts_f[0, :N_EXPERTS].astype(I32)
    padded = (counts + bm - 1) // bm * bm
    pad_end = jnp.cumsum(padded)
    pad_start = pad_end - padded
    n_blocks = -(-n_assign // bm) + N_EXPERTS
    e = info[:, 0:2].astype(I32)
    rank = info[:, 4:6].astype(I32)
    hit = e[:, :, None] == jnp.arange(N_EXPERTS, dtype=I32)[None, None, :]
    dest = (jnp.sum(jnp.where(hit, pad_start[None, None, :], 0), axis=-1) + rank).reshape(-1)
    return dest * TOKEN_ROWS, pad_start // bm, padded // bm, n_blocks * bm


def kernel(x, p, g_mix, w_in, conv_w, conv_b, lru_wa, lru_ba, lru_wx, lru_bx, lru_lambda, w_out, g_ffn, w_router_group, b_router_group, w_router_expert, b_router_expert, w1, w3, w2, g_ple, w_ple_gate, b_ple_gate, w_ple_proj, g_final):
    B, S, D = x.shape
    T = B * S
    depth = g_mix.shape[0]
    for l in range(depth):
        x1 = _mix(x, g_mix[l], w_in[l], conv_w[l], conv_b[l], lru_wa[l], lru_ba[l], lru_wx[l], lru_bx[l],
                  lru_lambda[l], w_out[l]).reshape(T, D)
        hft, info, counts = _route(x1, g_ffn[l], w_router_group[l], b_router_group[l],
                                   w_router_expert[l], b_router_expert[l])
        dest, first_block, block_count, n_slots = _slot_plan(info, counts, 2 * T)
        xs = _dispatch(dest, hft, n_slots)
        ys = _experts(first_block, block_count, xs, w1[l], w3[l], w2[l])
        x = _combine(dest, x1, p[l].reshape(T, -1), info, ys, g_ple[l], w_ple_gate[l], b_ple_gate[l],
                     w_ple_proj[l], g_final, l == depth - 1).reshape(B, S, D)
    return x
```

```python
import functools

import jax
import jax.numpy as jnp
from jax import lax
from jax.experimental import pallas as pl
from jax.experimental.pallas import tpu as pltpu

F32 = jnp.float32
BF16 = jnp.bfloat16
I32 = jnp.int32
U32 = jnp.uint32

EPS = 1e-6
LRU_C = 8.0
CONV_W = 4
RET_HEADS = 4
ROPE_BASE = 10000.0
N_GROUPS = 8
EXPERTS_PER_GROUP = 8
N_EXPERTS = N_GROUPS * EXPERTS_PER_GROUP

LANES = 128
SUBLANES = 8
VMEM_LIMIT = 56 * 1024 * 1024

MIX_TILE = 512
RET_CHUNK = 256
DISPATCH_TILE = 512
EXPERT_BLOCK = 256
COMBINE_TILE = 256
DMA_UNROLL = 8
GATHER_SLOTS = 3
IN_RING = 3
OUT_RING = 2


def _rms(x, g):
    ms = jnp.mean(x * x, axis=-1, keepdims=True)
    return (x * lax.rsqrt(ms + EPS)) * g


def _sigmoid(z):
    return 1.0 / (1.0 + jnp.exp(-z))


TOKEN_ROWS = 4


def _pack_pair(a, b):
    ab = lax.bitcast_convert_type(a.astype(BF16).astype(F32), U32)
    bb = lax.bitcast_convert_type(b.astype(BF16).astype(F32), U32)
    return ab | (bb >> 16)


def _unpack_pair(w):
    hi = lax.bitcast_convert_type(w & jnp.uint32(0xFFFF0000), F32)
    lo = lax.bitcast_convert_type(w << 16, F32)
    return hi, lo


def _store_token_tiles(tile_ref, x):
    n = x.shape[0]
    half = TOKEN_ROWS * LANES
    for j in range(TOKEN_ROWS):
        tile_ref[pl.ds(j, n, stride=TOKEN_ROWS), :] = _pack_pair(x[:, j * LANES:(j + 1) * LANES],
                                                                 x[:, half + j * LANES:half + (j + 1) * LANES])


def _load_token_pair(tile_ref, j, n):
    return _unpack_pair(tile_ref[pl.ds(j, n, stride=TOKEN_ROWS), :])


def _tile_copy(src_ref, src_row, dst_ref, dst_row, sem):
    return pltpu.make_async_copy(src_ref.at[pl.ds(pl.multiple_of(src_row, TOKEN_ROWS), TOKEN_ROWS), :],
                                 dst_ref.at[pl.ds(pl.multiple_of(dst_row, TOKEN_ROWS), TOKEN_ROWS), :], sem)


def _group_view(x):
    return x.reshape(x.shape[0] // SUBLANES, SUBLANES, x.shape[1])


def _shift_rows(x3, d, prev_group):
    r = pltpu.roll(x3, d, 1)
    prev = jnp.concatenate([pltpu.roll(prev_group[None], d, 1), r[:-1]], axis=0)
    sub = lax.broadcasted_iota(I32, x3.shape, 1)
    return jnp.where(sub < d, prev, r)


def _linear_scan(a, u, h0):
    a3, u3 = _group_view(a), _group_view(u)
    sub = lax.broadcasted_iota(I32, a3.shape, 1)
    d = 1
    while d < SUBLANES:
        keep = sub >= d
        a_sh = jnp.where(keep, pltpu.roll(a3, d, 1), 1.0)
        u_sh = jnp.where(keep, pltpu.roll(u3, d, 1), 0.0)
        u3 = u3 + a3 * u_sh
        a3 = a3 * a_sh
        d *= 2
    carry = h0
    groups = []
    for g in range(a3.shape[0]):
        hg = u3[g] + a3[g] * carry
        groups.append(hg)
        carry = hg[SUBLANES - 1:SUBLANES, :]
    return jnp.concatenate(groups, axis=0)


def _mix_kernel(x_ref, gmix_ref, win_ref, convw_ref, convb_ref, wg_ref, bg_ref, lam_ref,
                cos_ref, sin_ref, dmask_ref, qdec_ref, kdec_ref, cdec_ref, wout_ref, gffn_ref, wr_ref, br_ref,
                o_ref, info_ref, cnt_ref, xtail, qtail, hcar, state, ybuf, rcount):
    ts = x_ref.shape[1]
    d_lru = lam_ref.shape[1]
    d_ret = qdec_ref.shape[1]
    hd = d_ret // RET_HEADS
    chunk = dmask_ref.shape[1]
    half = d_lru // 2

    @pl.when((pl.program_id(0) == 0) & (pl.program_id(1) == 0))
    def _():
        rcount[...] = jnp.zeros_like(rcount)

    @pl.when(pl.program_id(1) == 0)
    def _():
        xtail[...] = jnp.zeros_like(xtail)
        qtail[...] = jnp.zeros_like(qtail)
        hcar[...] = jnp.zeros_like(hcar)
        state[...] = jnp.zeros_like(state)

    x = x_ref[0]
    h = _rms(x, gmix_ref[...]).astype(BF16)

    def proj(lo, width):
        return jnp.dot(h, win_ref[:, lo:lo + width], preferred_element_type=F32)

    xl = proj(0, d_lru)

    assert CONV_W == 4
    x3 = _group_view(xl)
    xm1 = _shift_rows(x3, 1, xtail[...])
    q3 = xm1 * convw_ref[0:1, :] + x3 * convw_ref[1:2, :]
    p3 = xm1 * convw_ref[2:3, :] + x3 * convw_ref[3:4, :]
    xc = ((convb_ref[...] + _shift_rows(q3, 2, qtail[...])) + p3).reshape(ts, d_lru)
    xtail[...] = x3[-1]
    qtail[...] = q3[-1]

    xcb = xc.astype(BF16)
    g0 = jnp.dot(xcb[:, :half], wg_ref[0], preferred_element_type=F32)
    g1 = jnp.dot(xcb[:, half:], wg_ref[1], preferred_element_type=F32)
    ra = jnp.concatenate([g0[:, :half], g1[:, :half]], axis=1) + bg_ref[:, :d_lru]
    ix = jnp.concatenate([g0[:, half:], g1[:, half:]], axis=1) + bg_ref[:, d_lru:]
    r = _sigmoid(ra)
    i_gate = _sigmoid(ix)
    z = -lam_ref[...]
    softplus = jnp.maximum(z, 0.0) + jnp.log1p(jnp.exp(-jnp.abs(z)))
    log_a = (-LRU_C * r) * softplus
    a = jnp.exp(log_a)
    v = 1.0 - a * a
    u = jnp.where(v > 0.0, v * lax.rsqrt(v), 0.0) * (i_gate * xc)
    hseq = _linear_scan(a, u, hcar[0:1, :])
    hcar[0:1, :] = hseq[ts - 1:ts, :]
    gl = proj(d_lru, d_lru)
    ybuf[:, 0:d_lru] = (hseq * jax.nn.gelu(gl)).astype(BF16)

    base = 2 * d_lru
    q = proj(base, d_ret)
    k = proj(base + d_ret, d_ret)
    v = proj(base + 2 * d_ret, d_ret)
    gr = proj(base + 3 * d_ret, d_ret)
    scale = hd ** -0.5
    for hh in range(RET_HEADS):
        ls = slice(hh * hd, (hh + 1) * hd)
        for c in range(ts // chunk):
            rs = slice(c * chunk, (c + 1) * chunk)
            cosv = cos_ref[rs, :]
            sinv = sin_ref[rs, :]
            qh = q[rs, ls]
            kh = k[rs, ls]
            qr = qh * cosv + pltpu.roll(qh, hd // 2, 1) * sinv
            kr = (kh * cosv + pltpu.roll(kh, hd // 2, 1) * sinv) * scale
            vb = v[rs, ls].astype(BF16)
            scores = lax.dot_general(qr.astype(BF16), kr.astype(BF16), (((1,), (1,)), ((), ())),
                                     preferred_element_type=F32)
            inner = jnp.dot((scores * dmask_ref[hh]).astype(BF16), vb, preferred_element_type=F32)
            st = state[hh]
            cross = jnp.dot((qr * qdec_ref[:, ls]).astype(BF16), st.astype(BF16), preferred_element_type=F32)
            kv = lax.dot_general((kr * kdec_ref[:, ls]).astype(BF16), vb, (((0,), (0,)), ((), ())),
                                 preferred_element_type=F32)
            state[hh] = st * cdec_ref[:, ls] + kv
            o = inner + cross
            mu = jnp.mean(o, axis=-1, keepdims=True)
            oc = o - mu
            var = jnp.mean(oc * oc, axis=-1, keepdims=True)
            on = oc * lax.rsqrt(var + EPS)
            grh = gr[rs, ls]
            ybuf[rs, d_lru + hh * hd:d_lru + (hh + 1) * hd] = ((grh * _sigmoid(grh)) * on).astype(BF16)

    x1 = x + jnp.dot(ybuf[...], wout_ref[...], preferred_element_type=F32)
    o_ref[0] = x1

    info_ref[0] = _route_tile(_rms(x1, gffn_ref[...]), wr_ref, br_ref, rcount)
    cnt_ref[...] = rcount[...]


def _mix(x, g_mix, w_in, conv_w, conv_b, lru_wa, lru_ba, lru_wx, lru_bx, lam, w_out, g_ffn, wr, br):
    B, S, D = x.shape
    assert B * S < (1 << 24)
    d_lru = lam.shape[0]
    d_ret = (w_in.shape[1] - 2 * d_lru) // 4
    hd = d_ret // RET_HEADS
    ts, chunk = MIX_TILE, RET_CHUNK
    assert S % ts == 0 and ts % chunk == 0 and d_lru % (2 * LANES) == 0 and hd == LANES
    nb, bw = lru_wa.shape[0], lru_wa.shape[1]
    half = d_lru // 2
    per_half = nb // 2

    def blockdiag(w):
        out = jnp.zeros((half, half), F32)
        for j in range(per_half):
            out = out.at[j * bw:(j + 1) * bw, j * bw:(j + 1) * bw].set(w[j])
        return out
    wg = jnp.stack([jnp.concatenate([blockdiag(lru_wa[hf * per_half:(hf + 1) * per_half]),
                                     blockdiag(lru_wx[hf * per_half:(hf + 1) * per_half])], axis=1)
                    for hf in range(2)]).astype(BF16)
    bg = jnp.concatenate([lru_ba.reshape(1, d_lru), lru_bx.reshape(1, d_lru)], axis=1)

    inv = ROPE_BASE ** (-jnp.arange(hd // 2, dtype=F32) / (hd // 2))
    ang = jnp.arange(S, dtype=F32)[:, None] * inv[None, :]
    cos2 = jnp.concatenate([jnp.cos(ang), jnp.cos(ang)], axis=1)
    sin2 = jnp.concatenate([-jnp.sin(ang), jnp.sin(ang)], axis=1)
    log_g = jnp.log(1.0 - 2.0 ** (-5.0 - jnp.arange(RET_HEADS, dtype=F32)))
    idx = jnp.arange(chunk, dtype=F32)
    diff = idx[:, None] - idx[None, :]
    dmask = jnp.where(diff >= 0, jnp.exp(jnp.maximum(diff, 0.0)[None] * log_g[:, None, None]), 0.0)
    rep = lambda t: jnp.repeat(t, hd, axis=-1)
    qdec = rep(jnp.exp((idx + 1.0)[:, None] * log_g[None, :]))
    kdec = rep(jnp.exp((chunk - 1.0 - idx)[:, None] * log_g[None, :]))
    cdec = rep(jnp.exp(chunk * log_g)[None, :])

    full = lambda shape: pl.BlockSpec(shape, lambda b, s: (0,) * len(shape))
    return pl.pallas_call(
        _mix_kernel,
        name="mix",
        grid=(B, S // ts),
        in_specs=[
            pl.BlockSpec((1, ts, D), lambda b, s: (b, s, 0)),
            full((1, D)),
            full(w_in.shape),
            full((CONV_W, d_lru)),
            full((1, d_lru)),
            full(wg.shape),
            full(bg.shape),
            full((1, d_lru)),
            pl.BlockSpec((ts, hd), lambda b, s: (s, 0)),
            pl.BlockSpec((ts, hd), lambda b, s: (s, 0)),
            full(dmask.shape),
            full(qdec.shape),
            full(kdec.shape),
            full(cdec.shape),
            full(w_out.shape),
            full((1, D)),
            full(wr.shape),
            full(br.shape),
        ],
        out_specs=[
            pl.BlockSpec((1, ts, D), lambda b, s: (b, s, 0)),
            pl.BlockSpec((1, ts, LANES), lambda b, s: (b, s, 0)),
            full((SUBLANES, LANES)),
        ],
        out_shape=[
            jax.ShapeDtypeStruct((B, S, D), F32),
            jax.ShapeDtypeStruct((B, S, LANES), F32),
            jax.ShapeDtypeStruct((SUBLANES, LANES), F32),
        ],
        scratch_shapes=[
            pltpu.VMEM((SUBLANES, d_lru), F32),
            pltpu.VMEM((SUBLANES, d_lru), F32),
            pltpu.VMEM((SUBLANES, d_lru), F32),
            pltpu.VMEM((RET_HEADS, hd, hd), F32),
            pltpu.VMEM((ts, d_lru + d_ret), BF16),
            pltpu.VMEM((SUBLANES, LANES), F32),
        ],
        compiler_params=pltpu.CompilerParams(
            dimension_semantics=("arbitrary", "arbitrary"), vmem_limit_bytes=VMEM_LIMIT),
    )(x, g_mix.reshape(1, D), w_in.astype(BF16), conv_w, conv_b.reshape(1, d_lru), wg, bg,
      lam.reshape(1, d_lru), cos2, sin2, dmask, qdec, kdec, cdec, w_out.astype(BF16), g_ffn.reshape(1, D), wr, br)


def _route_tile(hf, wr_ref, br_ref, carry):
    tm = hf.shape[0]
    h_hi = hf.astype(BF16)
    h_lo = (hf - h_hi.astype(F32)).astype(BF16)
    logits = (jnp.dot(h_hi, wr_ref[0], preferred_element_type=F32)
              + (jnp.dot(h_hi, wr_ref[1], preferred_element_type=F32)
                 + jnp.dot(h_lo, wr_ref[0], preferred_element_type=F32))) + br_ref[...]
    lane = lax.broadcasted_iota(I32, logits.shape, 1)
    neg = jnp.float32(-jnp.inf)
    gmask = lane < N_GROUPS
    gl = jnp.where(gmask, logits, neg)
    gmax = jnp.max(gl, axis=-1, keepdims=True)
    g_sel = jnp.min(jnp.where(gl == gmax, lane, LANES), axis=-1, keepdims=True)
    g_w = 1.0 / jnp.sum(jnp.where(gmask, jnp.exp(logits - gmax), 0.0), axis=-1, keepdims=True)

    e_lane = lane - N_GROUPS
    emask = (e_lane >= 0) & (e_lane < N_EXPERTS) & ((e_lane >> 3) == g_sel)
    el = jnp.where(emask, logits, neg)
    v1 = jnp.max(el, axis=-1, keepdims=True)
    i1 = jnp.min(jnp.where(el == v1, e_lane, LANES), axis=-1, keepdims=True)
    el2 = jnp.where(e_lane == i1, neg, el)
    v2 = jnp.max(el2, axis=-1, keepdims=True)
    i2 = jnp.min(jnp.where(el2 == v2, e_lane, LANES), axis=-1, keepdims=True)
    t = jnp.exp(v2 - v1)
    w1 = g_w / (1.0 + t)
    w2 = g_w * (t / (1.0 + t))

    hit = (lane == i1) | (lane == i2)
    onehot = jnp.where(hit, 1.0, 0.0)
    rr = lax.broadcasted_iota(I32, (tm, tm), 0)
    cc = lax.broadcasted_iota(I32, (tm, tm), 1)
    tri = jnp.where(rr > cc, 1.0, 0.0).astype(BF16)
    prefix = jnp.dot(tri, onehot.astype(BF16), preferred_element_type=F32) + carry[0:1, :]
    r1 = jnp.sum(jnp.where(lane == i1, prefix, 0.0), axis=-1, keepdims=True)
    r2 = jnp.sum(jnp.where(lane == i2, prefix, 0.0), axis=-1, keepdims=True)
    carry[0:1, :] = carry[0:1, :] + jnp.sum(onehot, axis=0, keepdims=True)

    info = jnp.where(lane == 0, i1.astype(F32), 0.0)
    info = jnp.where(lane == 1, i2.astype(F32), info)
    info = jnp.where(lane == 2, w1, info)
    info = jnp.where(lane == 3, w2, info)
    info = jnp.where(lane == 4, r1, info)
    info = jnp.where(lane == 5, r2, info)
    return info


def _router_weights(w_grp, b_grp, w_exp, b_exp):
    D = w_grp.shape[0]
    assert N_GROUPS + N_EXPERTS <= LANES
    pad = LANES - N_GROUPS - N_EXPERTS
    wr = jnp.concatenate([w_grp, w_exp, jnp.zeros((D, pad), F32)], axis=1)
    wr_hi = wr.astype(BF16)
    wr = jnp.stack([wr_hi, (wr - wr_hi.astype(F32)).astype(BF16)])
    br = jnp.concatenate([b_grp, b_exp, jnp.zeros((pad,), F32)]).reshape(1, LANES)
    return wr, br


def _zero_fill(start, padfirst_ref, padlen_ref, tail_ref, zbuf, xs_ref, zsem):
    bm = EXPERT_BLOCK

    def go(slot0, nslots):
        c = pltpu.make_async_copy(
            zbuf.at[pl.ds(0, nslots * TOKEN_ROWS), :],
            xs_ref.at[pl.ds(pl.multiple_of(slot0 * TOKEN_ROWS, TOKEN_ROWS), nslots * TOKEN_ROWS), :], zsem)
        if start:
            c.start()
        else:
            c.wait()

    def per_expert(e, carry):
        off = padfirst_ref[e]
        length = padlen_ref[e]
        for bit in reversed(range(bm.bit_length() - 1)):
            take = (length >> bit) & 1

            @pl.when(take == 1)
            def _():
                go(off, 1 << bit)
            off = off + (take << bit)
        return carry

    lax.fori_loop(0, N_EXPERTS, per_expert, 0)

    def per_tail(k, carry):
        go((tail_ref[0] + k) * bm, bm)
        return carry

    lax.fori_loop(0, tail_ref[1], per_tail, 0)


def _dispatch_kernel(padfirst_ref, padlen_ref, tail_ref, dest_ref, x1_ref, g_ref, xs_ref, tbuf, zbuf, sems, zsem):
    tm = x1_ref.shape[0]
    i = pl.program_id(0)
    s = lax.rem(i, 2)

    @pl.when(i == 0)
    def _():
        zbuf[...] = jnp.zeros_like(zbuf)
        _zero_fill(True, padfirst_ref, padlen_ref, tail_ref, zbuf, xs_ref, zsem)

    def wait_all(slot):
        for _ in range(2 * tm):
            _tile_copy(tbuf.at[slot], 0, xs_ref, 0, sems.at[slot]).wait()

    @pl.when(i >= 2)
    def _():
        wait_all(s)

    _store_token_tiles(tbuf.at[s], _rms(x1_ref[...], g_ref[...]))
    for t in range(tm):
        _tile_copy(tbuf.at[s], t * TOKEN_ROWS, xs_ref, dest_ref[2 * t], sems.at[s]).start(priority=0)
        _tile_copy(tbuf.at[s], t * TOKEN_ROWS, xs_ref, dest_ref[2 * t + 1], sems.at[s]).start(priority=1)

    @pl.when(i == pl.num_programs(0) - 1)
    def _():
        @pl.when(i >= 1)
        def _():
            wait_all(1 - s)
        wait_all(s)
        _zero_fill(False, padfirst_ref, padlen_ref, tail_ref, zbuf, xs_ref, zsem)


def _dispatch(pad_first, pad_len, tail, dest, x1, g_ffn, n_slots):
    T, D = x1.shape
    tm = DISPATCH_TILE
    assert T % tm == 0 and D == 2 * TOKEN_ROWS * LANES
    grid_spec = pltpu.PrefetchScalarGridSpec(
        num_scalar_prefetch=3,
        grid=(T // tm,),
        in_specs=[
            pl.BlockSpec((2 * tm,), lambda i, *_: (i,), memory_space=pltpu.SMEM),
            pl.BlockSpec((tm, D), lambda i, *_: (i, 0)),
            pl.BlockSpec((1, D), lambda i, *_: (0, 0)),
        ],
        out_specs=pl.BlockSpec(memory_space=pl.ANY),
        scratch_shapes=[
            pltpu.VMEM((2, tm * TOKEN_ROWS, LANES), U32),
            pltpu.VMEM((EXPERT_BLOCK * TOKEN_ROWS, LANES), U32),
            pltpu.SemaphoreType.DMA((2,)),
            pltpu.SemaphoreType.DMA,
        ],
    )
    return pl.pallas_call(
        _dispatch_kernel,
        name="dispatch",
        grid_spec=grid_spec,
        out_shape=jax.ShapeDtypeStruct((n_slots * TOKEN_ROWS, LANES), U32),
        compiler_params=pltpu.CompilerParams(
            dimension_semantics=("arbitrary",), has_side_effects=True, vmem_limit_bytes=VMEM_LIMIT),
    )(pad_first, pad_len, tail, dest, x1, g_ffn.reshape(1, D))


def _experts_kernel(first_ref, count_ref, xs_in_ref, w1_hbm, w3_hbm, w2_hbm, ys_ref,
                    w1f, w3f, w2f, w1b, w3b, w2b, xbuf, ybuf, xb, sem_w, sem_in, sem_out):
    del xs_in_ref
    e = pl.program_id(0)
    n_exp = pl.num_programs(0)
    bm = xb.shape[0]
    rows = bm * TOKEN_ROWS
    first = first_ref[e]
    count = count_ref[e]
    total = first_ref[n_exp - 1] + count_ref[n_exp - 1]
    wslot = lax.rem(e, 2)

    def weight_copies(ex, s):
        return [pltpu.make_async_copy(hbm.at[ex], buf.at[s], sem_w.at[s])
                for hbm, buf in ((w1_hbm, w1f), (w3_hbm, w3f), (w2_hbm, w2f))]

    def block_rows(g):
        return pl.ds(pl.multiple_of(g * rows, rows), rows)

    def in_copy(g):
        s = lax.rem(g, IN_RING)
        return pltpu.make_async_copy(ys_ref.at[block_rows(g), :], xbuf.at[s], sem_in.at[s])

    def out_copy(g):
        s = lax.rem(g, OUT_RING)
        return pltpu.make_async_copy(ybuf.at[s], ys_ref.at[block_rows(g), :], sem_out.at[s])

    @pl.when(e == 0)
    def _():
        for c in weight_copies(0, 0):
            c.start(priority=1)
        for k in range(IN_RING - 1):
            @pl.when(k < total)
            def _():
                in_copy(k).start()

    @pl.when(e + 1 < n_exp)
    def _():
        for c in weight_copies(e + 1, 1 - wslot):
            c.start(priority=1)

    for c in weight_copies(e, wslot):
        c.wait()
    w1b[...] = w1f[wslot].astype(BF16)
    w3b[...] = w3f[wslot].astype(BF16)
    w2b[...] = w2f[wslot].astype(BF16)

    def body(j, carry):
        g = first + j
        in_copy(g).wait()

        @pl.when(g + IN_RING - 1 < total)
        def _():
            in_copy(g + IN_RING - 1).start()

        @pl.when(g >= OUT_RING)
        def _():
            out_copy(g - OUT_RING).wait()

        s = lax.rem(g, IN_RING)
        half = TOKEN_ROWS * LANES
        for r in range(TOKEN_ROWS):
            hi, lo = _load_token_pair(xbuf.at[s], r, bm)
            xb[:, r * LANES:(r + 1) * LANES] = hi.astype(BF16)
            xb[:, half + r * LANES:half + (r + 1) * LANES] = lo.astype(BF16)
        x = xb[...]
        a1 = jnp.dot(x, w1b[...], preferred_element_type=F32)
        a3 = jnp.dot(x, w3b[...], preferred_element_type=F32)
        hmid = ((a1 * _sigmoid(a1)) * a3).astype(BF16)
        _store_token_tiles(ybuf.at[lax.rem(g, OUT_RING)], jnp.dot(hmid, w2b[...], preferred_element_type=F32))
        out_copy(g).start()
        return carry

    lax.fori_loop(0, count, body, 0)

    @pl.when(e == n_exp - 1)
    def _():
        for k in range(OUT_RING):
            @pl.when(total - 1 - k >= 0)
            def _():
                out_copy(total - 1 - k).wait()


def _experts(first_block, block_count, xs, w1, w3, w2):
    bm = EXPERT_BLOCK
    E, D, DE = w1.shape
    assert D == 2 * TOKEN_ROWS * LANES
    rows = bm * TOKEN_ROWS
    grid_spec = pltpu.PrefetchScalarGridSpec(
        num_scalar_prefetch=2,
        grid=(E,),
        in_specs=[
            pl.BlockSpec(memory_space=pl.ANY),
            pl.BlockSpec(memory_space=pl.ANY),
            pl.BlockSpec(memory_space=pl.ANY),
            pl.BlockSpec(memory_space=pl.ANY),
        ],
        out_specs=pl.BlockSpec(memory_space=pl.ANY),
        scratch_shapes=[
            pltpu.VMEM((2, D, DE), F32),
            pltpu.VMEM((2, D, DE), F32),
            pltpu.VMEM((2, DE, D), F32),
            pltpu.VMEM((D, DE), BF16),
            pltpu.VMEM((D, DE), BF16),
            pltpu.VMEM((DE, D), BF16),
            pltpu.VMEM((IN_RING, rows, LANES), U32),
            pltpu.VMEM((OUT_RING, rows, LANES), U32),
            pltpu.VMEM((bm, D), BF16),
            pltpu.SemaphoreType.DMA((2,)),
            pltpu.SemaphoreType.DMA((IN_RING,)),
            pltpu.SemaphoreType.DMA((OUT_RING,)),
        ],
    )
    return pl.pallas_call(
        _experts_kernel,
        name="experts",
        grid_spec=grid_spec,
        out_shape=jax.ShapeDtypeStruct(xs.shape, U32),
        input_output_aliases={2: 0},
        compiler_params=pltpu.CompilerParams(
            dimension_semantics=("arbitrary",), has_side_effects=True, vmem_limit_bytes=VMEM_LIMIT),
    )(first_block, block_count, xs, w1, w3, w2)


def _combine_kernel(dest0_ref, dest1_ref, dest2_ref, x1_ref, p_ref, info_ref, ys_ref, gple_ref, wgate_ref, bgate_ref,
                    wproj_ref, gfin_ref, o_ref, gbuf, sems, *, final_norm):
    tm = x1_ref.shape[0]
    i = pl.program_id(0)
    last = pl.num_programs(0) - 1
    slot = lax.rem(i, GATHER_SLOTS)
    ahead = lax.rem(i + GATHER_SLOTS - 1, GATHER_SLOTS)

    def start_pair(idx_ref, t, s):
        _tile_copy(ys_ref, idx_ref[2 * t], gbuf.at[s, 0], t * TOKEN_ROWS, sems.at[s]).start(priority=0)
        _tile_copy(ys_ref, idx_ref[2 * t + 1], gbuf.at[s, 1], t * TOKEN_ROWS, sems.at[s]).start(priority=1)

    def wait_all(s):
        for _ in range(2 * tm):
            _tile_copy(ys_ref, 0, gbuf.at[s, 0], 0, sems.at[s]).wait()

    @pl.when(i == 0)
    def _():
        def body(t, c):
            start_pair(dest0_ref, t, 0)
            start_pair(dest1_ref, t, 1)
            return c
        lax.fori_loop(0, tm, body, 0, unroll=DMA_UNROLL)

    wait_all(slot)
    info = info_ref[...]
    w_a = info[:, 2:3]
    w_b = info[:, 3:4]
    his, los = [], []
    for j in range(TOKEN_ROWS):
        a_hi, a_lo = _load_token_pair(gbuf.at[slot, 0], j, tm)
        b_hi, b_lo = _load_token_pair(gbuf.at[slot, 1], j, tm)
        his.append(a_hi * w_a + b_hi * w_b)
        los.append(a_lo * w_a + b_lo * w_b)
    x2 = x1_ref[...] + jnp.concatenate(his + los, axis=1)

    for t in range(tm):
        start_pair(dest2_ref, t, ahead)

    pp = jnp.dot(p_ref[...].astype(BF16), wproj_ref[...], preferred_element_type=F32)
    gate = _sigmoid(jnp.dot(_rms(x2, gple_ref[...]).astype(BF16), wgate_ref[...], preferred_element_type=F32)
                    + bgate_ref[...])
    x3 = x2 + gate * pp
    o_ref[...] = _rms(x3, gfin_ref[...]) if final_norm else x3

    @pl.when(i == last)
    def _():
        wait_all(lax.rem(i + 1, GATHER_SLOTS))
        wait_all(ahead)


def _combine(dest, x1, p, info, ys, g_ple, w_gate, b_gate, w_proj, g_final, final_norm):
    T, D = x1.shape
    PD = p.shape[1]
    tm = COMBINE_TILE
    assert T % tm == 0 and D == 2 * TOKEN_ROWS * LANES
    n_steps = T // tm
    const = lambda shape: pl.BlockSpec(shape, lambda i: (0,) * len(shape))
    return pl.pallas_call(
        functools.partial(_combine_kernel, final_norm=final_norm),
        name="combine",
        grid=(n_steps,),
        in_specs=[
            pl.BlockSpec((2 * tm,), lambda i: (i,), memory_space=pltpu.SMEM),
            pl.BlockSpec((2 * tm,), lambda i: (jnp.minimum(i + 1, n_steps - 1),), memory_space=pltpu.SMEM),
            pl.BlockSpec((2 * tm,), lambda i: (jnp.minimum(i + 2, n_steps - 1),), memory_space=pltpu.SMEM),
            pl.BlockSpec((tm, D), lambda i: (i, 0)),
            pl.BlockSpec((tm, PD), lambda i: (i, 0)),
            pl.BlockSpec((tm, LANES), lambda i: (i, 0)),
            pl.BlockSpec(memory_space=pl.ANY),
            const((1, D)),
            const((D, D)),
            const((1, D)),
            const((PD, D)),
            const((1, D)),
        ],
        out_specs=pl.BlockSpec((tm, D), lambda i: (i, 0)),
        out_shape=jax.ShapeDtypeStruct((T, D), F32),
        scratch_shapes=[pltpu.VMEM((GATHER_SLOTS, 2, tm * TOKEN_ROWS, LANES), U32),
                        pltpu.SemaphoreType.DMA((GATHER_SLOTS,))],
        compiler_params=pltpu.CompilerParams(
            dimension_semantics=("arbitrary",), vmem_limit_bytes=VMEM_LIMIT),
    )(dest, dest, dest, x1, p, info, ys, g_ple.reshape(1, D), w_gate.astype(BF16), b_gate.reshape(1, D),
      w_proj.astype(BF16), g_final.reshape(1, D))


def _slot_plan(info, counts_f, n_assign):
    bm = EXPERT_BLOCK
    counts = counts_f[0, :N_EXPERTS].astype(I32)
    padded = (counts + bm - 1) // bm * bm
    pad_end = jnp.cumsum(padded)
    pad_start = pad_end - padded
    n_blocks = -(-n_assign // bm) + N_EXPERTS
    e = info[:, 0:2].astype(I32)
    rank = info[:, 4:6].astype(I32)
    hit = e[:, :, None] == jnp.arange(N_EXPERTS, dtype=I32)[None, None, :]
    dest = (jnp.sum(jnp.where(hit, pad_start[None, None, :], 0), axis=-1) + rank).reshape(-1)
    tail = jnp.stack([pad_end[-1] // bm, n_blocks - pad_end[-1] // bm]).astype(I32)
    plan = dict(
        dest=dest * TOKEN_ROWS,
        first_block=pad_start // bm,
        block_count=padded // bm,
        pad_first=pad_start + counts,
        pad_len=padded - counts,
        tail=tail,
    )
    return plan, n_blocks * bm


def kernel(x, p, g_mix, w_in, conv_w, conv_b, lru_wa, lru_ba, lru_wx, lru_bx, lru_lambda, w_out, g_ffn, w_router_group, b_router_group, w_router_expert, b_router_expert, w1, w3, w2, g_ple, w_ple_gate, b_ple_gate, w_ple_proj, g_final):
    B, S, D = x.shape
    T = B * S
    depth = g_mix.shape[0]
    for l in range(depth):
        wr, br = _router_weights(w_router_group[l], b_router_group[l], w_router_expert[l], b_router_expert[l])
        x1, info, counts = _mix(x, g_mix[l], w_in[l], conv_w[l], conv_b[l], lru_wa[l], lru_ba[l], lru_wx[l],
                                lru_bx[l], lru_lambda[l], w_out[l], g_ffn[l], wr, br)
        x1 = x1.reshape(T, D)
        info = info.reshape(T, LANES)
        plan, n_slots = _slot_plan(info, counts, 2 * T)
        xs = _dispatch(plan["pad_first"], plan["pad_len"], plan["tail"], plan["dest"], x1, g_ffn[l], n_slots)
        ys = _experts(plan["first_block"], plan["block_count"], xs, w1[l], w3[l], w2[l])
        x = _combine(plan["dest"], x1, p[l].reshape(T, -1), info, ys, g_ple[l], w_ple_gate[l], b_ple_gate[l],
                     w_ple_proj[l], g_final, l == depth - 1).reshape(B, S, D)
    return x
```

```python
import functools

import jax
import jax.numpy as jnp
from jax import lax
from jax.experimental import pallas as pl
from jax.experimental.pallas import tpu as pltpu

F32 = jnp.float32
BF16 = jnp.bfloat16
I32 = jnp.int32
U32 = jnp.uint32

EPS = 1e-6
LRU_C = 8.0
CONV_W = 4
RET_HEADS = 4
ROPE_BASE = 10000.0
N_GROUPS = 8
EXPERTS_PER_GROUP = 8
N_EXPERTS = N_GROUPS * EXPERTS_PER_GROUP

LANES = 128
SUBLANES = 8
VMEM_LIMIT = 56 * 1024 * 1024

MIX_TILE = 512
RET_CHUNK = 256
DISPATCH_TILE = 512
EXPERT_BLOCK = 256
COMBINE_TILE = 256
DMA_UNROLL = 8
GATHER_SLOTS = 3
IN_RING = 3
OUT_RING = 2


def _rms(x, g):
    ms = jnp.mean(x * x, axis=-1, keepdims=True)
    return (x * lax.rsqrt(ms + EPS)) * g


def _sigmoid(z):
    return 1.0 / (1.0 + jnp.exp(-z))


TOKEN_ROWS = 4


def _pack_pair(a, b):
    ab = lax.bitcast_convert_type(a.astype(BF16).astype(F32), U32)
    bb = lax.bitcast_convert_type(b.astype(BF16).astype(F32), U32)
    return ab | (bb >> 16)


def _unpack_pair(w):
    hi = lax.bitcast_convert_type(w & jnp.uint32(0xFFFF0000), F32)
    lo = lax.bitcast_convert_type(w << 16, F32)
    return hi, lo


def _store_token_tiles(tile_ref, x):
    n = x.shape[0]
    half = TOKEN_ROWS * LANES
    for j in range(TOKEN_ROWS):
        tile_ref[pl.ds(j, n, stride=TOKEN_ROWS), :] = _pack_pair(x[:, j * LANES:(j + 1) * LANES],
                                                                 x[:, half + j * LANES:half + (j + 1) * LANES])


def _load_token_pair(tile_ref, j, n):
    return _unpack_pair(tile_ref[pl.ds(j, n, stride=TOKEN_ROWS), :])


def _tile_copy(src_ref, src_row, dst_ref, dst_row, sem):
    return pltpu.make_async_copy(src_ref.at[pl.ds(pl.multiple_of(src_row, TOKEN_ROWS), TOKEN_ROWS), :],
                                 dst_ref.at[pl.ds(pl.multiple_of(dst_row, TOKEN_ROWS), TOKEN_ROWS), :], sem)


def _group_view(x):
    return x.reshape(x.shape[0] // SUBLANES, SUBLANES, x.shape[1])


def _shift_rows(x3, d, prev_group):
    r = pltpu.roll(x3, d, 1)
    prev = jnp.concatenate([pltpu.roll(prev_group[None], d, 1), r[:-1]], axis=0)
    sub = lax.broadcasted_iota(I32, x3.shape, 1)
    return jnp.where(sub < d, prev, r)


def _linear_scan(a, u, h0):
    a3, u3 = _group_view(a), _group_view(u)
    sub = lax.broadcasted_iota(I32, a3.shape, 1)
    d = 1
    while d < SUBLANES:
        keep = sub >= d
        a_sh = jnp.where(keep, pltpu.roll(a3, d, 1), 1.0)
        u_sh = jnp.where(keep, pltpu.roll(u3, d, 1), 0.0)
        u3 = u3 + a3 * u_sh
        a3 = a3 * a_sh
        d *= 2
    carry = h0
    groups = []
    for g in range(a3.shape[0]):
        hg = u3[g] + a3[g] * carry
        groups.append(hg)
        carry = hg[SUBLANES - 1:SUBLANES, :]
    return jnp.concatenate(groups, axis=0)


def _mix_kernel(x_ref, gmix_ref, win_ref, convw_ref, convb_ref, wg_ref, bg_ref, lam_ref,
                cos_ref, sin_ref, dmask_ref, qdec_ref, kdec_ref, cdec_ref, wout_ref, gffn_ref, wrt_ref, brt_ref, tri_ref,
                o_ref, info_ref, cnt_ref, xtail, qtail, hcar, state, ybuf, rcount,
                *, tiles_per_seq):
    ts = x_ref.shape[1]
    d_lru = lam_ref.shape[1]
    d_ret = qdec_ref.shape[1]
    hd = d_ret // RET_HEADS
    chunk = dmask_ref.shape[1]
    half = d_lru // 2
    i = pl.program_id(0)

    @pl.when(i == 0)
    def _():
        rcount[...] = jnp.zeros_like(rcount)

    @pl.when(lax.rem(i, tiles_per_seq) == 0)
    def _():
        xtail[...] = jnp.zeros_like(xtail)
        qtail[...] = jnp.zeros_like(qtail)
        hcar[...] = jnp.zeros_like(hcar)
        state[...] = jnp.zeros_like(state)

    x = x_ref[0]
    h = _rms(x, gmix_ref[...]).astype(BF16)

    def proj(lo, width):
        return jnp.dot(h, win_ref[:, lo:lo + width], preferred_element_type=F32)

    xl = proj(0, d_lru)

    assert CONV_W == 4
    x3 = _group_view(xl)
    xm1 = _shift_rows(x3, 1, xtail[...])
    q3 = xm1 * convw_ref[0:1, :] + x3 * convw_ref[1:2, :]
    p3 = xm1 * convw_ref[2:3, :] + x3 * convw_ref[3:4, :]
    xc = ((convb_ref[...] + _shift_rows(q3, 2, qtail[...])) + p3).reshape(ts, d_lru)
    xtail[...] = x3[-1]
    qtail[...] = q3[-1]

    xcb = xc.astype(BF16)
    g0 = jnp.dot(xcb[:, :half], wg_ref[0], preferred_element_type=F32)
    g1 = jnp.dot(xcb[:, half:], wg_ref[1], preferred_element_type=F32)
    ra = jnp.concatenate([g0[:, :half], g1[:, :half]], axis=1) + bg_ref[:, :d_lru]
    ix = jnp.concatenate([g0[:, half:], g1[:, half:]], axis=1) + bg_ref[:, d_lru:]
    r = _sigmoid(ra)
    i_gate = _sigmoid(ix)
    z = -lam_ref[...]
    softplus = jnp.maximum(z, 0.0) + jnp.log1p(jnp.exp(-jnp.abs(z)))
    log_a = (-LRU_C * r) * softplus
    a = jnp.exp(log_a)
    v = 1.0 - a * a
    u = jnp.where(v > 0.0, v * lax.rsqrt(v), 0.0) * (i_gate * xc)
    hseq = _linear_scan(a, u, hcar[0:1, :])
    hcar[0:1, :] = hseq[ts - 1:ts, :]
    gl = proj(d_lru, d_lru)
    ybuf[:, 0:d_lru] = (hseq * jax.nn.gelu(gl)).astype(BF16)

    base = 2 * d_lru
    q = proj(base, d_ret)
    k = proj(base + d_ret, d_ret)
    v = proj(base + 2 * d_ret, d_ret)
    gr = proj(base + 3 * d_ret, d_ret)
    scale = hd ** -0.5
    for hh in range(RET_HEADS):
        ls = slice(hh * hd, (hh + 1) * hd)
        for c in range(ts // chunk):
            rs = slice(c * chunk, (c + 1) * chunk)
            cosv = cos_ref[rs, :]
            sinv = sin_ref[rs, :]
            qh = q[rs, ls]
            kh = k[rs, ls]
            qr = qh * cosv + pltpu.roll(qh, hd // 2, 1) * sinv
            kr = (kh * cosv + pltpu.roll(kh, hd // 2, 1) * sinv) * scale
            vb = v[rs, ls].astype(BF16)
            scores = lax.dot_general(qr.astype(BF16), kr.astype(BF16), (((1,), (1,)), ((), ())),
                                     preferred_element_type=F32)
            inner = jnp.dot((scores * dmask_ref[hh]).astype(BF16), vb, preferred_element_type=F32)
            st = state[hh]
            cross = jnp.dot((qr * qdec_ref[:, ls]).astype(BF16), st.astype(BF16), preferred_element_type=F32)
            kv = lax.dot_general((kr * kdec_ref[:, ls]).astype(BF16), vb, (((0,), (0,)), ((), ())),
                                 preferred_element_type=F32)
            state[hh] = st * cdec_ref[:, ls] + kv
            o = inner + cross
            mu = jnp.mean(o, axis=-1, keepdims=True)
            oc = o - mu
            var = jnp.mean(oc * oc, axis=-1, keepdims=True)
            on = oc * lax.rsqrt(var + EPS)
            grh = gr[rs, ls]
            ybuf[rs, d_lru + hh * hd:d_lru + (hh + 1) * hd] = ((grh * _sigmoid(grh)) * on).astype(BF16)

    x1 = x + jnp.dot(ybuf[...], wout_ref[...], preferred_element_type=F32)
    o_ref[0] = x1

    info_ref[...] = _route_tile(_rms(x1, gffn_ref[...]), wrt_ref, brt_ref, tri_ref, rcount)
    cnt_ref[...] = rcount[...]


def _mix(x, g_mix, w_in, conv_w, conv_b, lru_wa, lru_ba, lru_wx, lru_bx, lam, w_out, g_ffn, wrt, brt, tri):
    B, S, D = x.shape
    assert B * S < (1 << 24)
    d_lru = lam.shape[0]
    d_ret = (w_in.shape[1] - 2 * d_lru) // 4
    hd = d_ret // RET_HEADS
    ts, chunk = MIX_TILE, RET_CHUNK
    assert S % ts == 0 and ts % chunk == 0 and d_lru % (2 * LANES) == 0 and hd == LANES
    nb, bw = lru_wa.shape[0], lru_wa.shape[1]
    half = d_lru // 2
    per_half = nb // 2

    def blockdiag(w):
        out = jnp.zeros((half, half), F32)
        for j in range(per_half):
            out = out.at[j * bw:(j + 1) * bw, j * bw:(j + 1) * bw].set(w[j])
        return out
    wg = jnp.stack([jnp.concatenate([blockdiag(lru_wa[hf * per_half:(hf + 1) * per_half]),
                                     blockdiag(lru_wx[hf * per_half:(hf + 1) * per_half])], axis=1)
                    for hf in range(2)]).astype(BF16)
    bg = jnp.concatenate([lru_ba.reshape(1, d_lru), lru_bx.reshape(1, d_lru)], axis=1)

    inv = ROPE_BASE ** (-jnp.arange(hd // 2, dtype=F32) / (hd // 2))
    ang = jnp.arange(S, dtype=F32)[:, None] * inv[None, :]
    cos2 = jnp.concatenate([jnp.cos(ang), jnp.cos(ang)], axis=1)
    sin2 = jnp.concatenate([-jnp.sin(ang), jnp.sin(ang)], axis=1)
    log_g = jnp.log(1.0 - 2.0 ** (-5.0 - jnp.arange(RET_HEADS, dtype=F32)))
    idx = jnp.arange(chunk, dtype=F32)
    diff = idx[:, None] - idx[None, :]
    dmask = jnp.where(diff >= 0, jnp.exp(jnp.maximum(diff, 0.0)[None] * log_g[:, None, None]), 0.0)
    rep = lambda t: jnp.repeat(t, hd, axis=-1)
    qdec = rep(jnp.exp((idx + 1.0)[:, None] * log_g[None, :]))
    kdec = rep(jnp.exp((chunk - 1.0 - idx)[:, None] * log_g[None, :]))
    cdec = rep(jnp.exp(chunk * log_g)[None, :])

    tps = S // ts
    n_tiles = B * tps
    full = lambda shape: pl.BlockSpec(shape, lambda i: (0,) * len(shape))
    return pl.pallas_call(
        functools.partial(_mix_kernel, tiles_per_seq=tps),
        name="mix",
        grid=(n_tiles,),
        in_specs=[
            pl.BlockSpec((1, ts, D), lambda i: (i // tps, i % tps, 0)),
            full((1, D)),
            full(w_in.shape),
            full((CONV_W, d_lru)),
            full((1, d_lru)),
            full(wg.shape),
            full(bg.shape),
            full((1, d_lru)),
            pl.BlockSpec((ts, hd), lambda i: (i % tps, 0)),
            pl.BlockSpec((ts, hd), lambda i: (i % tps, 0)),
            full(dmask.shape),
            full(qdec.shape),
            full(kdec.shape),
            full(cdec.shape),
            full(w_out.shape),
            full((1, D)),
            full(wrt.shape),
            full(brt.shape),
            full(tri.shape),
        ],
        out_specs=[
            pl.BlockSpec((1, ts, D), lambda i: (i // tps, i % tps, 0)),
            pl.BlockSpec((SUBLANES, ts), lambda i: (0, i)),
            full((LANES, LANES)),
        ],
        out_shape=[
            jax.ShapeDtypeStruct((B, S, D), F32),
            jax.ShapeDtypeStruct((SUBLANES, B * S), F32),
            jax.ShapeDtypeStruct((LANES, LANES), F32),
        ],
        scratch_shapes=[
            pltpu.VMEM((SUBLANES, d_lru), F32),
            pltpu.VMEM((SUBLANES, d_lru), F32),
            pltpu.VMEM((SUBLANES, d_lru), F32),
            pltpu.VMEM((RET_HEADS, hd, hd), F32),
            pltpu.VMEM((ts, d_lru + d_ret), BF16),
            pltpu.VMEM((LANES, LANES), F32),
        ],
        compiler_params=pltpu.CompilerParams(
            dimension_semantics=("arbitrary",), vmem_limit_bytes=VMEM_LIMIT),
    )(x, g_mix.reshape(1, D), w_in.astype(BF16), conv_w, conv_b.reshape(1, d_lru), wg, bg,
      lam.reshape(1, d_lru), cos2, sin2, dmask, qdec, kdec, cdec, w_out.astype(BF16), g_ffn.reshape(1, D), wrt, brt, tri)


def _route_tile(hf, wrt_ref, brt_ref, tri_ref, carry):
    tm = hf.shape[0]
    nt = lambda a, b: lax.dot_general(a, b, (((1,), (1,)), ((), ())), preferred_element_type=F32)
    h_hi = hf.astype(BF16)
    h_lo = (hf - h_hi.astype(F32)).astype(BF16)
    both = nt(wrt_ref[...], h_hi)
    logits = (both[0:LANES] + (both[LANES:] + nt(wrt_ref[0:LANES, :], h_lo))) + brt_ref[...]
    row = lax.broadcasted_iota(I32, logits.shape, 0)
    neg = jnp.float32(-jnp.inf)
    g_row = row - N_EXPERTS
    gmask = (g_row >= 0) & (g_row < N_GROUPS)
    gl = jnp.where(gmask, logits, neg)
    gmax = jnp.max(gl, axis=0, keepdims=True)
    g_sel = jnp.min(jnp.where(gl == gmax, g_row, LANES), axis=0, keepdims=True)
    g_w = 1.0 / jnp.sum(jnp.where(gmask, jnp.exp(logits - gmax), 0.0), axis=0, keepdims=True)

    emask = (row < N_EXPERTS) & ((row >> 3) == g_sel)
    el = jnp.where(emask, logits, neg)
    v1 = jnp.max(el, axis=0, keepdims=True)
    i1 = jnp.min(jnp.where(el == v1, row, LANES), axis=0, keepdims=True)
    el2 = jnp.where(row == i1, neg, el)
    v2 = jnp.max(el2, axis=0, keepdims=True)
    i2 = jnp.min(jnp.where(el2 == v2, row, LANES), axis=0, keepdims=True)
    t = jnp.exp(v2 - v1)
    w1 = g_w / (1.0 + t)
    w2 = g_w * (t / (1.0 + t))

    onehot = jnp.where((row == i1) | (row == i2), 1.0, 0.0).astype(BF16)
    before = jnp.concatenate([carry[...]] * (tm // LANES), axis=1)
    prefix = jnp.dot(onehot, tri_ref[...], preferred_element_type=F32) + before
    r1 = jnp.sum(jnp.where(row == i1, prefix, 0.0), axis=0, keepdims=True)
    r2 = jnp.sum(jnp.where(row == i2, prefix, 0.0), axis=0, keepdims=True)
    carry[...] = carry[...] + jnp.dot(onehot, jnp.ones((tm, LANES), BF16), preferred_element_type=F32)

    zero = jnp.zeros_like(w1)
    return jnp.concatenate([i1.astype(F32), i2.astype(F32), w1, w2, r1, r2, zero, zero], axis=0)


def _router_weights(w_grp, b_grp, w_exp, b_exp, tm):
    D = w_grp.shape[0]
    assert N_GROUPS + N_EXPERTS <= LANES and EXPERTS_PER_GROUP == 8
    pad = LANES - N_GROUPS - N_EXPERTS
    wrt = jnp.concatenate([w_exp.T, w_grp.T, jnp.zeros((pad, D), F32)], axis=0)
    wrt_hi = wrt.astype(BF16)
    wrt = jnp.concatenate([wrt_hi, (wrt - wrt_hi.astype(F32)).astype(BF16)], axis=0)
    brt = jnp.broadcast_to(jnp.concatenate([b_exp, b_grp, jnp.zeros((pad,), F32)])[:, None], (LANES, tm))
    idx = jnp.arange(tm)
    tri = (idx[:, None] < idx[None, :]).astype(BF16)
    return wrt, brt, tri


def _zero_fill(start, padfirst_ref, padlen_ref, tail_ref, zbuf, xs_ref, zsem):
    bm = EXPERT_BLOCK

    def go(slot0, nslots):
        c = pltpu.make_async_copy(
            zbuf.at[pl.ds(0, nslots * TOKEN_ROWS), :],
            xs_ref.at[pl.ds(pl.multiple_of(slot0 * TOKEN_ROWS, TOKEN_ROWS), nslots * TOKEN_ROWS), :], zsem)
        if start:
            c.start()
        else:
            c.wait()

    def per_expert(e, carry):
        off = padfirst_ref[e]
        length = padlen_ref[e]
        for bit in reversed(range(bm.bit_length() - 1)):
            take = (length >> bit) & 1

            @pl.when(take == 1)
            def _():
                go(off, 1 << bit)
            off = off + (take << bit)
        return carry

    lax.fori_loop(0, N_EXPERTS, per_expert, 0)

    def per_tail(k, carry):
        go((tail_ref[0] + k) * bm, bm)
        return carry

    lax.fori_loop(0, tail_ref[1], per_tail, 0)


def _dispatch_kernel(padfirst_ref, padlen_ref, tail_ref, dest_ref, x1_ref, g_ref, xs_ref, tbuf, zbuf, sems, zsem):
    tm = x1_ref.shape[0]
    i = pl.program_id(0)
    s = lax.rem(i, 2)

    @pl.when(i == 0)
    def _():
        zbuf[...] = jnp.zeros_like(zbuf)
        _zero_fill(True, padfirst_ref, padlen_ref, tail_ref, zbuf, xs_ref, zsem)

    def wait_all(slot):
        for _ in range(2 * tm):
            _tile_copy(tbuf.at[slot], 0, xs_ref, 0, sems.at[slot]).wait()

    @pl.when(i >= 2)
    def _():
        wait_all(s)

    _store_token_tiles(tbuf.at[s], _rms(x1_ref[...], g_ref[...]))
    for t in range(tm):
        _tile_copy(tbuf.at[s], t * TOKEN_ROWS, xs_ref, dest_ref[2 * t], sems.at[s]).start(priority=0)
        _tile_copy(tbuf.at[s], t * TOKEN_ROWS, xs_ref, dest_ref[2 * t + 1], sems.at[s]).start(priority=1)

    @pl.when(i == pl.num_programs(0) - 1)
    def _():
        @pl.when(i >= 1)
        def _():
            wait_all(1 - s)
        wait_all(s)
        _zero_fill(False, padfirst_ref, padlen_ref, tail_ref, zbuf, xs_ref, zsem)


def _dispatch(pad_first, pad_len, tail, dest, x1, g_ffn, n_slots):
    T, D = x1.shape
    tm = DISPATCH_TILE
    assert T % tm == 0 and D == 2 * TOKEN_ROWS * LANES
    grid_spec = pltpu.PrefetchScalarGridSpec(
        num_scalar_prefetch=3,
        grid=(T // tm,),
        in_specs=[
            pl.BlockSpec((2 * tm,), lambda i, *_: (i,), memory_space=pltpu.SMEM),
            pl.BlockSpec((tm, D), lambda i, *_: (i, 0)),
            pl.BlockSpec((1, D), lambda i, *_: (0, 0)),
        ],
        out_specs=pl.BlockSpec(memory_space=pl.ANY),
        scratch_shapes=[
            pltpu.VMEM((2, tm * TOKEN_ROWS, LANES), U32),
            pltpu.VMEM((EXPERT_BLOCK * TOKEN_ROWS, LANES), U32),
            pltpu.SemaphoreType.DMA((2,)),
            pltpu.SemaphoreType.DMA,
        ],
    )
    return pl.pallas_call(
        _dispatch_kernel,
        name="dispatch",
        grid_spec=grid_spec,
        out_shape=jax.ShapeDtypeStruct((n_slots * TOKEN_ROWS, LANES), U32),
        compiler_params=pltpu.CompilerParams(
            dimension_semantics=("arbitrary",), has_side_effects=True, vmem_limit_bytes=VMEM_LIMIT),
    )(pad_first, pad_len, tail, dest, x1, g_ffn.reshape(1, D))


def _experts_kernel(first_ref, count_ref, xs_in_ref, w1_hbm, w3_hbm, w2_hbm, ys_ref,
                    w1f, w3f, w2f, w1b, w3b, w2b, xbuf, ybuf, xb, sem_w, sem_in, sem_out):
    del xs_in_ref
    e = pl.program_id(0)
    n_exp = pl.num_programs(0)
    bm = xb.shape[0]
    rows = bm * TOKEN_ROWS
    first = first_ref[e]
    count = count_ref[e]
    total = first_ref[n_exp - 1] + count_ref[n_exp - 1]
    wslot = lax.rem(e, 2)

    def weight_copies(ex, s):
        return [pltpu.make_async_copy(hbm.at[ex], buf.at[s], sem_w.at[s])
                for hbm, buf in ((w1_hbm, w1f), (w3_hbm, w3f), (w2_hbm, w2f))]

    def block_rows(g):
        return pl.ds(pl.multiple_of(g * rows, rows), rows)

    def in_copy(g):
        s = lax.rem(g, IN_RING)
        return pltpu.make_async_copy(ys_ref.at[block_rows(g), :], xbuf.at[s], sem_in.at[s])

    def out_copy(g):
        s = lax.rem(g, OUT_RING)
        return pltpu.make_async_copy(ybuf.at[s], ys_ref.at[block_rows(g), :], sem_out.at[s])

    @pl.when(e == 0)
    def _():
        for c in weight_copies(0, 0):
            c.start(priority=1)
        for k in range(IN_RING - 1):
            @pl.when(k < total)
            def _():
                in_copy(k).start()

    @pl.when(e + 1 < n_exp)
    def _():
        for c in weight_copies(e + 1, 1 - wslot):
            c.start(priority=1)

    for c in weight_copies(e, wslot):
        c.wait()
    w1b[...] = w1f[wslot].astype(BF16)
    w3b[...] = w3f[wslot].astype(BF16)
    w2b[...] = w2f[wslot].astype(BF16)

    def body(j, carry):
        g = first + j
        in_copy(g).wait()

        @pl.when(g + IN_RING - 1 < total)
        def _():
            in_copy(g + IN_RING - 1).start()

        @pl.when(g >= OUT_RING)
        def _():
            out_copy(g - OUT_RING).wait()

        s = lax.rem(g, IN_RING)
        half = TOKEN_ROWS * LANES
        for r in range(TOKEN_ROWS):
            hi, lo = _load_token_pair(xbuf.at[s], r, bm)
            xb[:, r * LANES:(r + 1) * LANES] = hi.astype(BF16)
            xb[:, half + r * LANES:half + (r + 1) * LANES] = lo.astype(BF16)
        x = xb[...]
        a1 = jnp.dot(x, w1b[...], preferred_element_type=F32)
        a3 = jnp.dot(x, w3b[...], preferred_element_type=F32)
        hmid = ((a1 * _sigmoid(a1)) * a3).astype(BF16)
        _store_token_tiles(ybuf.at[lax.rem(g, OUT_RING)], jnp.dot(hmid, w2b[...], preferred_element_type=F32))
        out_copy(g).start()
        return carry

    lax.fori_loop(0, count, body, 0)

    @pl.when(e == n_exp - 1)
    def _():
        for k in range(OUT_RING):
            @pl.when(total - 1 - k >= 0)
            def _():
                out_copy(total - 1 - k).wait()


def _experts(first_block, block_count, xs, w1, w3, w2):
    bm = EXPERT_BLOCK
    E, D, DE = w1.shape
    assert D == 2 * TOKEN_ROWS * LANES
    rows = bm * TOKEN_ROWS
    grid_spec = pltpu.PrefetchScalarGridSpec(
        num_scalar_prefetch=2,
        grid=(E,),
        in_specs=[
            pl.BlockSpec(memory_space=pl.ANY),
            pl.BlockSpec(memory_space=pl.ANY),
            pl.BlockSpec(memory_space=pl.ANY),
            pl.BlockSpec(memory_space=pl.ANY),
        ],
        out_specs=pl.BlockSpec(memory_space=pl.ANY),
        scratch_shapes=[
            pltpu.VMEM((2, D, DE), F32),
            pltpu.VMEM((2, D, DE), F32),
            pltpu.VMEM((2, DE, D), F32),
            pltpu.VMEM((D, DE), BF16),
            pltpu.VMEM((D, DE), BF16),
            pltpu.VMEM((DE, D), BF16),
            pltpu.VMEM((IN_RING, rows, LANES), U32),
            pltpu.VMEM((OUT_RING, rows, LANES), U32),
            pltpu.VMEM((bm, D), BF16),
            pltpu.SemaphoreType.DMA((2,)),
            pltpu.SemaphoreType.DMA((IN_RING,)),
            pltpu.SemaphoreType.DMA((OUT_RING,)),
        ],
    )
    return pl.pallas_call(
        _experts_kernel,
        name="experts",
        grid_spec=grid_spec,
        out_shape=jax.ShapeDtypeStruct(xs.shape, U32),
        input_output_aliases={2: 0},
        compiler_params=pltpu.CompilerParams(
            dimension_semantics=("arbitrary",), has_side_effects=True, vmem_limit_bytes=VMEM_LIMIT),
    )(first_block, block_count, xs, w1, w3, w2)


def _combine_kernel(dest0_ref, dest1_ref, dest2_ref, x1_ref, p_ref, info_ref, ys_ref, gple_ref, wgate_ref, bgate_ref,
                    wproj_ref, gfin_ref, o_ref, gbuf, sems, *, final_norm):
    tm = x1_ref.shape[0]
    i = pl.program_id(0)
    last = pl.num_programs(0) - 1
    slot = lax.rem(i, GATHER_SLOTS)
    ahead = lax.rem(i + GATHER_SLOTS - 1, GATHER_SLOTS)

    def start_pair(idx_ref, t, s):
        _tile_copy(ys_ref, idx_ref[2 * t], gbuf.at[s, 0], t * TOKEN_ROWS, sems.at[s]).start(priority=0)
        _tile_copy(ys_ref, idx_ref[2 * t + 1], gbuf.at[s, 1], t * TOKEN_ROWS, sems.at[s]).start(priority=1)

    def wait_all(s):
        for _ in range(2 * tm):
            _tile_copy(ys_ref, 0, gbuf.at[s, 0], 0, sems.at[s]).wait()

    @pl.when(i == 0)
    def _():
        def body(t, c):
            start_pair(dest0_ref, t, 0)
            start_pair(dest1_ref, t, 1)
            return c
        lax.fori_loop(0, tm, body, 0, unroll=DMA_UNROLL)

    wait_all(slot)
    info = info_ref[...].T
    w_a = info[:, 2:3]
    w_b = info[:, 3:4]
    his, los = [], []
    for j in range(TOKEN_ROWS):
        a_hi, a_lo = _load_token_pair(gbuf.at[slot, 0], j, tm)
        b_hi, b_lo = _load_token_pair(gbuf.at[slot, 1], j, tm)
        his.append(a_hi * w_a + b_hi * w_b)
        los.append(a_lo * w_a + b_lo * w_b)
    x2 = x1_ref[...] + jnp.concatenate(his + los, axis=1)

    for t in range(tm):
        start_pair(dest2_ref, t, ahead)

    pp = jnp.dot(p_ref[...].astype(BF16), wproj_ref[...], preferred_element_type=F32)
    gate = _sigmoid(jnp.dot(_rms(x2, gple_ref[...]).astype(BF16), wgate_ref[...], preferred_element_type=F32)
                    + bgate_ref[...])
    x3 = x2 + gate * pp
    o_ref[...] = _rms(x3, gfin_ref[...]) if final_norm else x3

    @pl.when(i == last)
    def _():
        wait_all(lax.rem(i + 1, GATHER_SLOTS))
        wait_all(ahead)


def _combine(dest, x1, p, layer, info, ys, g_ple, w_gate, b_gate, w_proj, g_final, final_norm):
    T, D = x1.shape
    PD = p.shape[1]
    tm = COMBINE_TILE
    assert T % tm == 0 and D == 2 * TOKEN_ROWS * LANES
    n_steps = T // tm
    const = lambda shape: pl.BlockSpec(shape, lambda i: (0,) * len(shape))
    return pl.pallas_call(
        functools.partial(_combine_kernel, final_norm=final_norm),
        name="combine",
        grid=(n_steps,),
        in_specs=[
            pl.BlockSpec((2 * tm,), lambda i: (i,), memory_space=pltpu.SMEM),
            pl.BlockSpec((2 * tm,), lambda i: (jnp.minimum(i + 1, n_steps - 1),), memory_space=pltpu.SMEM),
            pl.BlockSpec((2 * tm,), lambda i: (jnp.minimum(i + 2, n_steps - 1),), memory_space=pltpu.SMEM),
            pl.BlockSpec((tm, D), lambda i: (i, 0)),
            pl.BlockSpec((tm, PD), lambda i: (layer * n_steps + i, 0)),
            pl.BlockSpec((SUBLANES, tm), lambda i: (0, i)),
            pl.BlockSpec(memory_space=pl.ANY),
            const((1, D)),
            const((D, D)),
            const((1, D)),
            const((PD, D)),
            const((1, D)),
        ],
        out_specs=pl.BlockSpec((tm, D), lambda i: (i, 0)),
        out_shape=jax.ShapeDtypeStruct((T, D), F32),
        scratch_shapes=[pltpu.VMEM((GATHER_SLOTS, 2, tm * TOKEN_ROWS, LANES), U32),
                        pltpu.SemaphoreType.DMA((GATHER_SLOTS,))],
        compiler_params=pltpu.CompilerParams(
            dimension_semantics=("arbitrary",), vmem_limit_bytes=VMEM_LIMIT),
    )(dest, dest, dest, x1, p, info, ys, g_ple.reshape(1, D), w_gate.astype(BF16), b_gate.reshape(1, D),
      w_proj.astype(BF16), g_final.reshape(1, D))


def _slot_plan(info_t, counts_f, n_assign):
    bm = EXPERT_BLOCK
    counts = counts_f[:N_EXPERTS, 0].astype(I32)
    padded = (counts + bm - 1) // bm * bm
    pad_end = jnp.cumsum(padded)
    pad_start = pad_end - padded
    n_blocks = -(-n_assign // bm) + N_EXPERTS
    e = info_t[0:2, :].astype(I32)
    rank = info_t[4:6, :].astype(I32)
    hit = e[:, :, None] == jnp.arange(N_EXPERTS, dtype=I32)[None, None, :]
    dest = (jnp.sum(jnp.where(hit, pad_start[None, None, :], 0), axis=-1) + rank).T.reshape(-1)
    tail = jnp.stack([pad_end[-1] // bm, n_blocks - pad_end[-1] // bm]).astype(I32)
    plan = dict(
        dest=dest * TOKEN_ROWS,
        first_block=pad_start // bm,
        block_count=padded // bm,
        pad_first=pad_start + counts,
        pad_len=padded - counts,
        tail=tail,
    )
    return plan, n_blocks * bm


def kernel(x, p, g_mix, w_in, conv_w, conv_b, lru_wa, lru_ba, lru_wx, lru_bx, lru_lambda, w_out, g_ffn, w_router_group, b_router_group, w_router_expert, b_router_expert, w1, w3, w2, g_ple, w_ple_gate, b_ple_gate, w_ple_proj, g_final):
    B, S, D = x.shape
    T = B * S
    depth = g_mix.shape[0]
    for l in range(depth):
        wrt, brt, tri = _router_weights(w_router_group[l], b_router_group[l], w_router_expert[l],
                                        b_router_expert[l], MIX_TILE)
        x1, info_t, counts = _mix(x, g_mix[l], w_in[l], conv_w[l], conv_b[l], lru_wa[l], lru_ba[l], lru_wx[l],
                                  lru_bx[l], lru_lambda[l], w_out[l], g_ffn[l], wrt, brt, tri)
        x1 = x1.reshape(T, D)
        plan, n_slots = _slot_plan(info_t, counts, 2 * T)
        xs = _dispatch(plan["pad_first"], plan["pad_len"], plan["tail"], plan["dest"], x1, g_ffn[l], n_slots)
        ys = _experts(plan["first_block"], plan["block_count"], xs, w1[l], w3[l], w2[l])
        x = _combine(plan["dest"], x1, p.reshape(depth * T, -1), l, info_t, ys, g_ple[l], w_ple_gate[l], b_ple_gate[l],
                     w_ple_proj[l], g_final, l == depth - 1).reshape(B, S, D)
    return x
```

```python
import functools

import jax
import jax.numpy as jnp
from jax import lax
from jax.experimental import pallas as pl
from jax.experimental.pallas import tpu as pltpu

F32 = jnp.float32
BF16 = jnp.bfloat16
I32 = jnp.int32
U32 = jnp.uint32

EPS = 1e-6
LRU_C = 8.0
CONV_W = 4
RET_HEADS = 4
ROPE_BASE = 10000.0
N_GROUPS = 8
EXPERTS_PER_GROUP = 8
N_EXPERTS = N_GROUPS * EXPERTS_PER_GROUP

LANES = 128
SUBLANES = 8
VMEM_LIMIT = 56 * 1024 * 1024

MIX_TILE = 512
RET_CHUNK = 256
SLOT_TILE = 512
DISPATCH_TILE = SLOT_TILE
COMBINE_TILE = SLOT_TILE
EXPERT_BLOCK = 256
DMA_UNROLL = 8
GATHER_SLOTS = 3
IN_RING = 3
OUT_RING = 2


def _rms(x, g):
    ms = jnp.mean(x * x, axis=-1, keepdims=True)
    return (x * lax.rsqrt(ms + EPS)) * g


def _sigmoid(z):
    return 1.0 / (1.0 + jnp.exp(-z))


TOKEN_ROWS = 4


def _pack_pair(a, b):
    ab = lax.bitcast_convert_type(a.astype(BF16).astype(F32), U32)
    bb = lax.bitcast_convert_type(b.astype(BF16).astype(F32), U32)
    return ab | (bb >> 16)


def _unpack_pair(w):
    hi = lax.bitcast_convert_type(w & jnp.uint32(0xFFFF0000), F32)
    lo = lax.bitcast_convert_type(w << 16, F32)
    return hi, lo


def _store_token_tiles(tile_ref, x):
    n = x.shape[0]
    half = TOKEN_ROWS * LANES
    for j in range(TOKEN_ROWS):
        tile_ref[pl.ds(j, n, stride=TOKEN_ROWS), :] = _pack_pair(x[:, j * LANES:(j + 1) * LANES],
                                                                 x[:, half + j * LANES:half + (j + 1) * LANES])


def _load_token_pair(tile_ref, j, n):
    return _unpack_pair(tile_ref[pl.ds(j, n, stride=TOKEN_ROWS), :])


def _tile_copy(src_ref, src_row, dst_ref, dst_row, sem):
    return pltpu.make_async_copy(src_ref.at[pl.ds(pl.multiple_of(src_row, TOKEN_ROWS), TOKEN_ROWS), :],
                                 dst_ref.at[pl.ds(pl.multiple_of(dst_row, TOKEN_ROWS), TOKEN_ROWS), :], sem)


def _group_view(x):
    return x.reshape(x.shape[0] // SUBLANES, SUBLANES, x.shape[1])


def _shift_rows(x3, d, prev_group):
    r = pltpu.roll(x3, d, 1)
    prev = jnp.concatenate([pltpu.roll(prev_group[None], d, 1), r[:-1]], axis=0)
    sub = lax.broadcasted_iota(I32, x3.shape, 1)
    return jnp.where(sub < d, prev, r)


def _linear_scan(a, u, h0):
    a3, u3 = _group_view(a), _group_view(u)
    sub = lax.broadcasted_iota(I32, a3.shape, 1)
    d = 1
    while d < SUBLANES:
        keep = sub >= d
        a_sh = jnp.where(keep, pltpu.roll(a3, d, 1), 1.0)
        u_sh = jnp.where(keep, pltpu.roll(u3, d, 1), 0.0)
        u3 = u3 + a3 * u_sh
        a3 = a3 * a_sh
        d *= 2
    carry = h0
    groups = []
    for g in range(a3.shape[0]):
        hg = u3[g] + a3[g] * carry
        groups.append(hg)
        carry = hg[SUBLANES - 1:SUBLANES, :]
    return jnp.concatenate(groups, axis=0)


def _mix_kernel(x_ref, gmix_ref, win_ref, convw_ref, convb_ref, wg_ref, bg_ref, lam_ref,
                cos_ref, sin_ref, dmask_ref, qdec_ref, kdec_ref, cdec_ref, wout_ref, gffn_ref, wrt_ref, brt_ref, tri_ref,
                o_ref, info_ref, cnt_ref, xtail, qtail, hcar, state, ybuf, rcount,
                *, tiles_per_seq):
    ts = x_ref.shape[1]
    d_lru = lam_ref.shape[1]
    d_ret = qdec_ref.shape[1]
    hd = d_ret // RET_HEADS
    chunk = dmask_ref.shape[1]
    half = d_lru // 2
    i = pl.program_id(0)

    @pl.when(i == 0)
    def _():
        rcount[...] = jnp.zeros_like(rcount)

    @pl.when(lax.rem(i, tiles_per_seq) == 0)
    def _():
        xtail[...] = jnp.zeros_like(xtail)
        qtail[...] = jnp.zeros_like(qtail)
        hcar[...] = jnp.zeros_like(hcar)
        state[...] = jnp.zeros_like(state)

    x = x_ref[0]
    h = _rms(x, gmix_ref[...]).astype(BF16)

    def proj(lo, width):
        return jnp.dot(h, win_ref[:, lo:lo + width], preferred_element_type=F32)

    xl = proj(0, d_lru)

    assert CONV_W == 4
    x3 = _group_view(xl)
    xm1 = _shift_rows(x3, 1, xtail[...])
    q3 = xm1 * convw_ref[0:1, :] + x3 * convw_ref[1:2, :]
    p3 = xm1 * convw_ref[2:3, :] + x3 * convw_ref[3:4, :]
    xc = ((convb_ref[...] + _shift_rows(q3, 2, qtail[...])) + p3).reshape(ts, d_lru)
    xtail[...] = x3[-1]
    qtail[...] = q3[-1]

    xcb = xc.astype(BF16)
    g0 = jnp.dot(xcb[:, :half], wg_ref[0], preferred_element_type=F32)
    g1 = jnp.dot(xcb[:, half:], wg_ref[1], preferred_element_type=F32)
    ra = jnp.concatenate([g0[:, :half], g1[:, :half]], axis=1) + bg_ref[:, :d_lru]
    ix = jnp.concatenate([g0[:, half:], g1[:, half:]], axis=1) + bg_ref[:, d_lru:]
    r = _sigmoid(ra)
    i_gate = _sigmoid(ix)
    z = -lam_ref[...]
    softplus = jnp.maximum(z, 0.0) + jnp.log1p(jnp.exp(-jnp.abs(z)))
    log_a = (-LRU_C * r) * softplus
    a = jnp.exp(log_a)
    om = 1.0 - a * a
    u = jnp.where(om > 0.0, om * lax.rsqrt(om), 0.0) * (i_gate * xc)
    hseq = _linear_scan(a, u, hcar[0:1, :])
    hcar[0:1, :] = hseq[ts - 1:ts, :]
    gl = proj(d_lru, d_lru)
    ybuf[:, 0:d_lru] = (hseq * jax.nn.gelu(gl)).astype(BF16)

    base = 2 * d_lru
    q = proj(base, d_ret)
    k = proj(base + d_ret, d_ret)
    v = proj(base + 2 * d_ret, d_ret)
    gr = proj(base + 3 * d_ret, d_ret)
    scale = hd ** -0.5
    for hh in range(RET_HEADS):
        ls = slice(hh * hd, (hh + 1) * hd)
        for c in range(ts // chunk):
            rs = slice(c * chunk, (c + 1) * chunk)
            cosv = cos_ref[rs, :]
            sinv = sin_ref[rs, :]
            qh = q[rs, ls]
            kh = k[rs, ls]
            qr = qh * cosv + pltpu.roll(qh, hd // 2, 1) * sinv
            kr = (kh * cosv + pltpu.roll(kh, hd // 2, 1) * sinv) * scale
            vb = v[rs, ls].astype(BF16)
            scores = lax.dot_general(qr.astype(BF16), kr.astype(BF16), (((1,), (1,)), ((), ())),
                                     preferred_element_type=F32)
            inner = jnp.dot((scores * dmask_ref[hh]).astype(BF16), vb, preferred_element_type=F32)
            st = state[hh]
            cross = jnp.dot((qr * qdec_ref[:, ls]).astype(BF16), st.astype(BF16), preferred_element_type=F32)
            kv = lax.dot_general((kr * kdec_ref[:, ls]).astype(BF16), vb, (((0,), (0,)), ((), ())),
                                 preferred_element_type=F32)
            state[hh] = st * cdec_ref[:, ls] + kv
            o = inner + cross
            mu = jnp.mean(o, axis=-1, keepdims=True)
            oc = o - mu
            var = jnp.mean(oc * oc, axis=-1, keepdims=True)
            on = oc * lax.rsqrt(var + EPS)
            grh = gr[rs, ls]
            ybuf[rs, d_lru + hh * hd:d_lru + (hh + 1) * hd] = ((grh * _sigmoid(grh)) * on).astype(BF16)

    x1 = x + jnp.dot(ybuf[...], wout_ref[...], preferred_element_type=F32)
    o_ref[0] = x1

    info_ref[...] = _route_tile(_rms(x1, gffn_ref[...]), wrt_ref, brt_ref, tri_ref, rcount)
    cnt_ref[...] = rcount[...]


def _mix(x, g_mix, w_in, conv_w, conv_b, lru_wa, lru_ba, lru_wx, lru_bx, lam, w_out, g_ffn, wrt, brt, tri):
    B, S, D = x.shape
    assert B * S < (1 << 24)
    d_lru = lam.shape[0]
    d_ret = (w_in.shape[1] - 2 * d_lru) // 4
    hd = d_ret // RET_HEADS
    ts, chunk = MIX_TILE, RET_CHUNK
    assert S % ts == 0 and ts % chunk == 0 and d_lru % (2 * LANES) == 0 and hd == LANES
    nb, bw = lru_wa.shape[0], lru_wa.shape[1]
    half = d_lru // 2
    per_half = nb // 2

    def blockdiag(w):
        out = jnp.zeros((half, half), F32)
        for j in range(per_half):
            out = out.at[j * bw:(j + 1) * bw, j * bw:(j + 1) * bw].set(w[j])
        return out
    wg = jnp.stack([jnp.concatenate([blockdiag(lru_wa[hf * per_half:(hf + 1) * per_half]),
                                     blockdiag(lru_wx[hf * per_half:(hf + 1) * per_half])], axis=1)
                    for hf in range(2)]).astype(BF16)
    bg = jnp.concatenate([lru_ba.reshape(1, d_lru), lru_bx.reshape(1, d_lru)], axis=1)

    inv = ROPE_BASE ** (-jnp.arange(hd // 2, dtype=F32) / (hd // 2))
    ang = jnp.arange(S, dtype=F32)[:, None] * inv[None, :]
    cos2 = jnp.concatenate([jnp.cos(ang), jnp.cos(ang)], axis=1)
    sin2 = jnp.concatenate([-jnp.sin(ang), jnp.sin(ang)], axis=1)
    log_g = jnp.log(1.0 - 2.0 ** (-5.0 - jnp.arange(RET_HEADS, dtype=F32)))
    idx = jnp.arange(chunk, dtype=F32)
    diff = idx[:, None] - idx[None, :]
    dmask = jnp.where(diff >= 0, jnp.exp(jnp.maximum(diff, 0.0)[None] * log_g[:, None, None]), 0.0)
    rep = lambda t: jnp.repeat(t, hd, axis=-1)
    qdec = rep(jnp.exp((idx + 1.0)[:, None] * log_g[None, :]))
    kdec = rep(jnp.exp((chunk - 1.0 - idx)[:, None] * log_g[None, :]))
    cdec = rep(jnp.exp(chunk * log_g)[None, :])

    tps = S // ts
    n_tiles = B * tps
    full = lambda shape: pl.BlockSpec(shape, lambda i: (0,) * len(shape))
    return pl.pallas_call(
        functools.partial(_mix_kernel, tiles_per_seq=tps),
        name="mix",
        grid=(n_tiles,),
        in_specs=[
            pl.BlockSpec((1, ts, D), lambda i: (i // tps, i % tps, 0)),
            full((1, D)),
            full(w_in.shape),
            full((CONV_W, d_lru)),
            full((1, d_lru)),
            full(wg.shape),
            full(bg.shape),
            full((1, d_lru)),
            pl.BlockSpec((ts, hd), lambda i: (i % tps, 0)),
            pl.BlockSpec((ts, hd), lambda i: (i % tps, 0)),
            full(dmask.shape),
            full(qdec.shape),
            full(kdec.shape),
            full(cdec.shape),
            full(w_out.shape),
            full((1, D)),
            full(wrt.shape),
            full(brt.shape),
            full(tri.shape),
        ],
        out_specs=[
            pl.BlockSpec((1, ts, D), lambda i: (i // tps, i % tps, 0)),
            pl.BlockSpec((SUBLANES, ts), lambda i: (0, i)),
            full((LANES, LANES)),
        ],
        out_shape=[
            jax.ShapeDtypeStruct((B, S, D), F32),
            jax.ShapeDtypeStruct((SUBLANES, B * S), F32),
            jax.ShapeDtypeStruct((LANES, LANES), F32),
        ],
        scratch_shapes=[
            pltpu.VMEM((SUBLANES, d_lru), F32),
            pltpu.VMEM((SUBLANES, d_lru), F32),
            pltpu.VMEM((SUBLANES, d_lru), F32),
            pltpu.VMEM((RET_HEADS, hd, hd), F32),
            pltpu.VMEM((ts, d_lru + d_ret), BF16),
            pltpu.VMEM((LANES, LANES), F32),
        ],
        compiler_params=pltpu.CompilerParams(
            dimension_semantics=("arbitrary",), vmem_limit_bytes=VMEM_LIMIT),
    )(x, g_mix.reshape(1, D), w_in.astype(BF16), conv_w, conv_b.reshape(1, d_lru), wg, bg,
      lam.reshape(1, d_lru), cos2, sin2, dmask, qdec, kdec, cdec, w_out.astype(BF16), g_ffn.reshape(1, D), wrt, brt, tri)


def _route_tile(hf, wrt_ref, brt_ref, tri_ref, carry):
    tm = hf.shape[0]
    nt = lambda a, b: lax.dot_general(a, b, (((1,), (1,)), ((), ())), preferred_element_type=F32)
    h_hi = hf.astype(BF16)
    h_lo = (hf - h_hi.astype(F32)).astype(BF16)
    both = nt(wrt_ref[...], h_hi)
    logits = (both[0:LANES] + (both[LANES:] + nt(wrt_ref[0:LANES, :], h_lo))) + brt_ref[...]
    row = lax.broadcasted_iota(I32, logits.shape, 0)
    neg = jnp.float32(-jnp.inf)
    g_row = row - N_EXPERTS
    gmask = (g_row >= 0) & (g_row < N_GROUPS)
    gl = jnp.where(gmask, logits, neg)
    gmax = jnp.max(gl, axis=0, keepdims=True)
    g_sel = jnp.min(jnp.where(gl == gmax, g_row, LANES), axis=0, keepdims=True)
    g_w = 1.0 / jnp.sum(jnp.where(gmask, jnp.exp(logits - gmax), 0.0), axis=0, keepdims=True)

    emask = (row < N_EXPERTS) & ((row >> 3) == g_sel)
    el = jnp.where(emask, logits, neg)
    v1 = jnp.max(el, axis=0, keepdims=True)
    i1 = jnp.min(jnp.where(el == v1, row, LANES), axis=0, keepdims=True)
    el2 = jnp.where(row == i1, neg, el)
    v2 = jnp.max(el2, axis=0, keepdims=True)
    i2 = jnp.min(jnp.where(el2 == v2, row, LANES), axis=0, keepdims=True)
    t = jnp.exp(v2 - v1)
    w1 = g_w / (1.0 + t)
    w2 = g_w * (t / (1.0 + t))

    onehot = jnp.where((row == i1) | (row == i2), 1.0, 0.0).astype(BF16)
    before = jnp.concatenate([carry[...]] * (tm // LANES), axis=1)
    prefix = jnp.dot(onehot, tri_ref[...], preferred_element_type=F32) + before
    r1 = jnp.sum(jnp.where(row == i1, prefix, 0.0), axis=0, keepdims=True)
    r2 = jnp.sum(jnp.where(row == i2, prefix, 0.0), axis=0, keepdims=True)
    carry[...] = carry[...] + jnp.dot(onehot, jnp.ones((tm, LANES), BF16), preferred_element_type=F32)

    zero = jnp.zeros_like(w1)
    return jnp.concatenate([i1.astype(F32), i2.astype(F32), w1, w2, r1, r2, zero, zero], axis=0)


def _router_weights(w_grp, b_grp, w_exp, b_exp, tm):
    D = w_grp.shape[0]
    assert N_GROUPS + N_EXPERTS <= LANES and EXPERTS_PER_GROUP == 8
    pad = LANES - N_GROUPS - N_EXPERTS
    wrt = jnp.concatenate([w_exp.T, w_grp.T, jnp.zeros((pad, D), F32)], axis=0)
    wrt_hi = wrt.astype(BF16)
    wrt = jnp.concatenate([wrt_hi, (wrt - wrt_hi.astype(F32)).astype(BF16)], axis=0)
    brt = jnp.broadcast_to(jnp.concatenate([b_exp, b_grp, jnp.zeros((pad,), F32)])[:, None], (LANES, tm))
    idx = jnp.arange(tm)
    tri = (idx[:, None] < idx[None, :]).astype(BF16)
    return wrt, brt, tri


def _zero_fill(start, padfirst_ref, padlen_ref, tail_ref, zbuf, xs_ref, zsem):
    bm = EXPERT_BLOCK

    def go(slot0, nslots):
        c = pltpu.make_async_copy(
            zbuf.at[pl.ds(0, nslots * TOKEN_ROWS), :],
            xs_ref.at[pl.ds(pl.multiple_of(slot0 * TOKEN_ROWS, TOKEN_ROWS), nslots * TOKEN_ROWS), :], zsem)
        if start:
            c.start()
        else:
            c.wait()

    def per_expert(e, carry):
        off = padfirst_ref[e]
        length = padlen_ref[e]
        for bit in reversed(range(bm.bit_length() - 1)):
            take = (length >> bit) & 1

            @pl.when(take == 1)
            def _():
                go(off, 1 << bit)
            off = off + (take << bit)
        return carry

    lax.fori_loop(0, N_EXPERTS, per_expert, 0)

    def per_tail(k, carry):
        go((tail_ref[0] + k) * bm, bm)
        return carry

    lax.fori_loop(0, tail_ref[1], per_tail, 0)


def _dispatch_kernel(padfirst_ref, padlen_ref, tail_ref, dest_ref, x1_ref, g_ref, xs_ref, tbuf, zbuf, sems, zsem):
    tm = x1_ref.shape[0]
    i = pl.program_id(0)
    s = lax.rem(i, 2)

    @pl.when(i == 0)
    def _():
        zbuf[...] = jnp.zeros_like(zbuf)
        _zero_fill(True, padfirst_ref, padlen_ref, tail_ref, zbuf, xs_ref, zsem)

    def wait_all(slot):
        for _ in range(2 * tm):
            _tile_copy(tbuf.at[slot], 0, xs_ref, 0, sems.at[slot]).wait()

    @pl.when(i >= 2)
    def _():
        wait_all(s)

    _store_token_tiles(tbuf.at[s], _rms(x1_ref[...], g_ref[...]))
    for t in range(tm):
        _tile_copy(tbuf.at[s], t * TOKEN_ROWS, xs_ref, dest_ref[t], sems.at[s]).start(priority=0)
        _tile_copy(tbuf.at[s], t * TOKEN_ROWS, xs_ref, dest_ref[tm + t], sems.at[s]).start(priority=1)

    @pl.when(i == pl.num_programs(0) - 1)
    def _():
        @pl.when(i >= 1)
        def _():
            wait_all(1 - s)
        wait_all(s)
        _zero_fill(False, padfirst_ref, padlen_ref, tail_ref, zbuf, xs_ref, zsem)


def _dispatch(pad_first, pad_len, tail, dest, x1, g_ffn, n_slots):
    T, D = x1.shape
    tm = DISPATCH_TILE
    assert T % tm == 0 and D == 2 * TOKEN_ROWS * LANES
    grid_spec = pltpu.PrefetchScalarGridSpec(
        num_scalar_prefetch=3,
        grid=(T // tm,),
        in_specs=[
            pl.BlockSpec((2 * tm,), lambda i, *_: (i,), memory_space=pltpu.SMEM),
            pl.BlockSpec((tm, D), lambda i, *_: (i, 0)),
            pl.BlockSpec((1, D), lambda i, *_: (0, 0)),
        ],
        out_specs=pl.BlockSpec(memory_space=pl.ANY),
        scratch_shapes=[
            pltpu.VMEM((2, tm * TOKEN_ROWS, LANES), U32),
            pltpu.VMEM((EXPERT_BLOCK * TOKEN_ROWS, LANES), U32),
            pltpu.SemaphoreType.DMA((2,)),
            pltpu.SemaphoreType.DMA,
        ],
    )
    return pl.pallas_call(
        _dispatch_kernel,
        name="dispatch",
        grid_spec=grid_spec,
        out_shape=jax.ShapeDtypeStruct((n_slots * TOKEN_ROWS, LANES), U32),
        compiler_params=pltpu.CompilerParams(
            dimension_semantics=("arbitrary",), has_side_effects=True, vmem_limit_bytes=VMEM_LIMIT),
    )(pad_first, pad_len, tail, dest, x1, g_ffn.reshape(1, D))


def _experts_kernel(first_ref, count_ref, xs_in_ref, w1_hbm, w3_hbm, w2_hbm, ys_ref,
                    w1f, w3f, w2f, w1b, w3b, w2b, xbuf, ybuf, xb, sem_w, sem_in, sem_out):
    del xs_in_ref
    e = pl.program_id(0)
    n_exp = pl.num_programs(0)
    bm = xb.shape[0]
    rows = bm * TOKEN_ROWS
    first = first_ref[e]
    count = count_ref[e]
    total = first_ref[n_exp - 1] + count_ref[n_exp - 1]
    wslot = lax.rem(e, 2)

    def weight_copies(ex, s):
        return [pltpu.make_async_copy(hbm.at[ex], buf.at[s], sem_w.at[s])
                for hbm, buf in ((w1_hbm, w1f), (w3_hbm, w3f), (w2_hbm, w2f))]

    def block_rows(g):
        return pl.ds(pl.multiple_of(g * rows, rows), rows)

    def in_copy(g):
        s = lax.rem(g, IN_RING)
        return pltpu.make_async_copy(ys_ref.at[block_rows(g), :], xbuf.at[s], sem_in.at[s])

    def out_copy(g):
        s = lax.rem(g, OUT_RING)
        return pltpu.make_async_copy(ybuf.at[s], ys_ref.at[block_rows(g), :], sem_out.at[s])

    @pl.when(e == 0)
    def _():
        for c in weight_copies(0, 0):
            c.start(priority=1)
        for k in range(IN_RING - 1):
            @pl.when(k < total)
            def _():
                in_copy(k).start()

    @pl.when(e + 1 < n_exp)
    def _():
        for c in weight_copies(e + 1, 1 - wslot):
            c.start(priority=1)

    for c in weight_copies(e, wslot):
        c.wait()
    w1b[...] = w1f[wslot].astype(BF16)
    w3b[...] = w3f[wslot].astype(BF16)
    w2b[...] = w2f[wslot].astype(BF16)

    def body(j, carry):
        g = first + j
        in_copy(g).wait()

        @pl.when(g + IN_RING - 1 < total)
        def _():
            in_copy(g + IN_RING - 1).start()

        @pl.when(g >= OUT_RING)
        def _():
            out_copy(g - OUT_RING).wait()

        s = lax.rem(g, IN_RING)
        half = TOKEN_ROWS * LANES
        for r in range(TOKEN_ROWS):
            hi, lo = _load_token_pair(xbuf.at[s], r, bm)
            xb[:, r * LANES:(r + 1) * LANES] = hi.astype(BF16)
            xb[:, half + r * LANES:half + (r + 1) * LANES] = lo.astype(BF16)
        x = xb[...]
        a1 = jnp.dot(x, w1b[...], preferred_element_type=F32)
        a3 = jnp.dot(x, w3b[...], preferred_element_type=F32)
        hmid = ((a1 * _sigmoid(a1)) * a3).astype(BF16)
        _store_token_tiles(ybuf.at[lax.rem(g, OUT_RING)], jnp.dot(hmid, w2b[...], preferred_element_type=F32))
        out_copy(g).start()
        return carry

    lax.fori_loop(0, count, body, 0)

    @pl.when(e == n_exp - 1)
    def _():
        for k in range(OUT_RING):
            @pl.when(total - 1 - k >= 0)
            def _():
                out_copy(total - 1 - k).wait()


def _experts(first_block, block_count, xs, w1, w3, w2):
    bm = EXPERT_BLOCK
    E, D, DE = w1.shape
    assert D == 2 * TOKEN_ROWS * LANES
    rows = bm * TOKEN_ROWS
    grid_spec = pltpu.PrefetchScalarGridSpec(
        num_scalar_prefetch=2,
        grid=(E,),
        in_specs=[
            pl.BlockSpec(memory_space=pl.ANY),
            pl.BlockSpec(memory_space=pl.ANY),
            pl.BlockSpec(memory_space=pl.ANY),
            pl.BlockSpec(memory_space=pl.ANY),
        ],
        out_specs=pl.BlockSpec(memory_space=pl.ANY),
        scratch_shapes=[
            pltpu.VMEM((2, D, DE), F32),
            pltpu.VMEM((2, D, DE), F32),
            pltpu.VMEM((2, DE, D), F32),
            pltpu.VMEM((D, DE), BF16),
            pltpu.VMEM((D, DE), BF16),
            pltpu.VMEM((DE, D), BF16),
            pltpu.VMEM((IN_RING, rows, LANES), U32),
            pltpu.VMEM((OUT_RING, rows, LANES), U32),
            pltpu.VMEM((bm, D), BF16),
            pltpu.SemaphoreType.DMA((2,)),
            pltpu.SemaphoreType.DMA((IN_RING,)),
            pltpu.SemaphoreType.DMA((OUT_RING,)),
        ],
    )
    return pl.pallas_call(
        _experts_kernel,
        name="experts",
        grid_spec=grid_spec,
        out_shape=jax.ShapeDtypeStruct(xs.shape, U32),
        input_output_aliases={2: 0},
        compiler_params=pltpu.CompilerParams(
            dimension_semantics=("arbitrary",), has_side_effects=True, vmem_limit_bytes=VMEM_LIMIT),
    )(first_block, block_count, xs, w1, w3, w2)


def _combine_kernel(dest0_ref, dest1_ref, dest2_ref, x1_ref, p_ref, info_ref, ys_ref, gple_ref, wgate_ref, bgate_ref,
                    wproj_ref, gfin_ref, o_ref, gbuf, sems, *, final_norm):
    tm = x1_ref.shape[0]
    i = pl.program_id(0)
    last = pl.num_programs(0) - 1
    slot = lax.rem(i, GATHER_SLOTS)
    ahead = lax.rem(i + GATHER_SLOTS - 1, GATHER_SLOTS)

    def start_pair(idx_ref, t, s):
        _tile_copy(ys_ref, idx_ref[t], gbuf.at[s, 0], t * TOKEN_ROWS, sems.at[s]).start(priority=0)
        _tile_copy(ys_ref, idx_ref[tm + t], gbuf.at[s, 1], t * TOKEN_ROWS, sems.at[s]).start(priority=1)

    def wait_all(s):
        for _ in range(2 * tm):
            _tile_copy(ys_ref, 0, gbuf.at[s, 0], 0, sems.at[s]).wait()

    @pl.when(i == 0)
    def _():
        def body(t, c):
            start_pair(dest0_ref, t, 0)
            start_pair(dest1_ref, t, 1)
            return c
        lax.fori_loop(0, tm, body, 0, unroll=DMA_UNROLL)

    wait_all(slot)
    info = info_ref[...].T
    w_a = info[:, 2:3]
    w_b = info[:, 3:4]
    his, los = [], []
    for j in range(TOKEN_ROWS):
        a_hi, a_lo = _load_token_pair(gbuf.at[slot, 0], j, tm)
        b_hi, b_lo = _load_token_pair(gbuf.at[slot, 1], j, tm)
        his.append(a_hi * w_a + b_hi * w_b)
        los.append(a_lo * w_a + b_lo * w_b)
    x2 = x1_ref[...] + jnp.concatenate(his + los, axis=1)

    for t in range(tm):
        start_pair(dest2_ref, t, ahead)

    pp = jnp.dot(p_ref[...].astype(BF16), wproj_ref[...], preferred_element_type=F32)
    gate = _sigmoid(jnp.dot(_rms(x2, gple_ref[...]).astype(BF16), wgate_ref[...], preferred_element_type=F32)
                    + bgate_ref[...])
    x3 = x2 + gate * pp
    o_ref[...] = _rms(x3, gfin_ref[...]) if final_norm else x3

    @pl.when(i == last)
    def _():
        wait_all(lax.rem(i + 1, GATHER_SLOTS))
        wait_all(ahead)


def _combine(dest, x1, p, layer, info, ys, g_ple, w_gate, b_gate, w_proj, g_final, final_norm):
    T, D = x1.shape
    S, PD = p.shape[2], p.shape[3]
    tm = COMBINE_TILE
    assert T % tm == 0 and S % tm == 0 and D == 2 * TOKEN_ROWS * LANES
    n_steps = T // tm
    tps = S // tm
    const = lambda shape: pl.BlockSpec(shape, lambda i: (0,) * len(shape))
    return pl.pallas_call(
        functools.partial(_combine_kernel, final_norm=final_norm),
        name="combine",
        grid=(n_steps,),
        in_specs=[
            pl.BlockSpec((2 * tm,), lambda i: (i,), memory_space=pltpu.SMEM),
            pl.BlockSpec((2 * tm,), lambda i: (jnp.minimum(i + 1, n_steps - 1),), memory_space=pltpu.SMEM),
            pl.BlockSpec((2 * tm,), lambda i: (jnp.minimum(i + 2, n_steps - 1),), memory_space=pltpu.SMEM),
            pl.BlockSpec((tm, D), lambda i: (i, 0)),
            pl.BlockSpec((None, None, tm, PD), lambda i: (layer, i // tps, i % tps, 0)),
            pl.BlockSpec((SUBLANES, tm), lambda i: (0, i)),
            pl.BlockSpec(memory_space=pl.ANY),
            const((1, D)),
            const((D, D)),
            const((1, D)),
            const((PD, D)),
            const((1, D)),
        ],
        out_specs=pl.BlockSpec((tm, D), lambda i: (i, 0)),
        out_shape=jax.ShapeDtypeStruct((T, D), F32),
        scratch_shapes=[pltpu.VMEM((GATHER_SLOTS, 2, tm * TOKEN_ROWS, LANES), U32),
                        pltpu.SemaphoreType.DMA((GATHER_SLOTS,))],
        compiler_params=pltpu.CompilerParams(
            dimension_semantics=("arbitrary",), vmem_limit_bytes=VMEM_LIMIT),
    )(dest, dest, dest, x1, p, info, ys, g_ple.reshape(1, D), w_gate.astype(BF16), b_gate.reshape(1, D),
      w_proj.astype(BF16), g_final.reshape(1, D))


def _slot_plan(info_t, counts_f, n_assign):
    bm = EXPERT_BLOCK
    counts = counts_f[:N_EXPERTS, 0].astype(I32)
    padded = (counts + bm - 1) // bm * bm
    pad_end = jnp.cumsum(padded)
    pad_start = pad_end - padded
    n_blocks = -(-n_assign // bm) + N_EXPERTS
    e = info_t[0:2, :].astype(I32)
    rank = info_t[4:6, :].astype(I32)
    hit = e[:, :, None] == jnp.arange(N_EXPERTS, dtype=I32)[None, None, :]
    dest = jnp.sum(jnp.where(hit, pad_start[None, None, :], 0), axis=-1) + rank
    T = dest.shape[1]
    dest = dest.reshape(2, T // SLOT_TILE, SLOT_TILE).transpose(1, 0, 2).reshape(-1)
    tail = jnp.stack([pad_end[-1] // bm, n_blocks - pad_end[-1] // bm]).astype(I32)
    plan = dict(
        dest=dest * TOKEN_ROWS,
        first_block=pad_start // bm,
        block_count=padded // bm,
        pad_first=pad_start + counts,
        pad_len=padded - counts,
        tail=tail,
    )
    return plan, n_blocks * bm


def kernel(x, p, g_mix, w_in, conv_w, conv_b, lru_wa, lru_ba, lru_wx, lru_bx, lru_lambda, w_out, g_ffn, w_router_group, b_router_group, w_router_expert, b_router_expert, w1, w3, w2, g_ple, w_ple_gate, b_ple_gate, w_ple_proj, g_final):
    B, S, D = x.shape
    T = B * S
    depth = g_mix.shape[0]
    for l in range(depth):
        wrt, brt, tri = _router_weights(w_router_group[l], b_router_group[l], w_router_expert[l],
                                        b_router_expert[l], MIX_TILE)
        x1, info_t, counts = _mix(x, g_mix[l], w_in[l], conv_w[l], conv_b[l], lru_wa[l], lru_ba[l], lru_wx[l],
                                  lru_bx[l], lru_lambda[l], w_out[l], g_ffn[l], wrt, brt, tri)
        x1 = x1.reshape(T, D)
        plan, n_slots = _slot_plan(info_t, counts, 2 * T)
        xs = _dispatch(plan["pad_first"], plan["pad_len"], plan["tail"], plan["dest"], x1, g_ffn[l], n_slots)
        ys = _experts(plan["first_block"], plan["block_count"], xs, w1[l], w3[l], w2[l])
        x = _combine(plan["dest"], x1, p, l, info_t, ys, g_ple[l], w_ple_gate[l], b_ple_gate[l],
                     w_ple_proj[l], g_final, l == depth - 1).reshape(B, S, D)
    return x
```

```python
import functools

import jax
import jax.numpy as jnp
from jax import lax
from jax.experimental import pallas as pl
from jax.experimental.pallas import tpu as pltpu

F32 = jnp.float32
BF16 = jnp.bfloat16
I32 = jnp.int32
U32 = jnp.uint32

EPS = 1e-6
LRU_C = 8.0
CONV_W = 4
RET_HEADS = 4
ROPE_BASE = 10000.0
N_GROUPS = 8
EXPERTS_PER_GROUP = 8
N_EXPERTS = N_GROUPS * EXPERTS_PER_GROUP

LANES = 128
SUBLANES = 8
VMEM_LIMIT = 56 * 1024 * 1024

MIX_TILE = 1024
RET_CHUNK = 256
SLOT_TILE = 512
DISPATCH_TILE = SLOT_TILE
COMBINE_TILE = SLOT_TILE
EXPERT_BLOCK = 256
DMA_UNROLL = 8
GATHER_SLOTS = 3
IN_RING = 3
OUT_RING = 2


def _rms(x, g):
    ms = jnp.mean(x * x, axis=-1, keepdims=True)
    return (x * lax.rsqrt(ms + EPS)) * g


def _sigmoid(z):
    return 1.0 / (1.0 + jnp.exp(-z))


TOKEN_ROWS = 4


def _pack_pair(a, b):
    ab = lax.bitcast_convert_type(a.astype(BF16).astype(F32), U32)
    bb = lax.bitcast_convert_type(b.astype(BF16).astype(F32), U32)
    return ab | (bb >> 16)


def _unpack_pair(w):
    hi = lax.bitcast_convert_type(w & jnp.uint32(0xFFFF0000), F32)
    lo = lax.bitcast_convert_type(w << 16, F32)
    return hi, lo


def _store_token_tiles(tile_ref, x):
    n = x.shape[0]
    half = TOKEN_ROWS * LANES
    for j in range(TOKEN_ROWS):
        tile_ref[pl.ds(j, n, stride=TOKEN_ROWS), :] = _pack_pair(x[:, j * LANES:(j + 1) * LANES],
                                                                 x[:, half + j * LANES:half + (j + 1) * LANES])


def _load_token_pair(tile_ref, j, n):
    return _unpack_pair(tile_ref[pl.ds(j, n, stride=TOKEN_ROWS), :])


def _tile_copy(src_ref, src_row, dst_ref, dst_row, sem):
    return pltpu.make_async_copy(src_ref.at[pl.ds(pl.multiple_of(src_row, TOKEN_ROWS), TOKEN_ROWS), :],
                                 dst_ref.at[pl.ds(pl.multiple_of(dst_row, TOKEN_ROWS), TOKEN_ROWS), :], sem)


def _group_view(x):
    return x.reshape(x.shape[0] // SUBLANES, SUBLANES, x.shape[1])


def _shift_rows(x3, d, prev_group):
    r = pltpu.roll(x3, d, 1)
    prev = jnp.concatenate([pltpu.roll(prev_group[None], d, 1), r[:-1]], axis=0)
    sub = lax.broadcasted_iota(I32, x3.shape, 1)
    return jnp.where(sub < d, prev, r)


def _linear_scan(a, u, h0):
    a3, u3 = _group_view(a), _group_view(u)
    sub = lax.broadcasted_iota(I32, a3.shape, 1)
    d = 1
    while d < SUBLANES:
        keep = sub >= d
        a_sh = jnp.where(keep, pltpu.roll(a3, d, 1), 1.0)
        u_sh = jnp.where(keep, pltpu.roll(u3, d, 1), 0.0)
        u3 = u3 + a3 * u_sh
        a3 = a3 * a_sh
        d *= 2
    carry = h0
    groups = []
    for g in range(a3.shape[0]):
        hg = u3[g] + a3[g] * carry
        groups.append(hg)
        carry = hg[SUBLANES - 1:SUBLANES, :]
    return jnp.concatenate(groups, axis=0)


def _mix_kernel(x_ref, gmix_ref, win_ref, convw_ref, convb_ref, wg_ref, bg_ref, lam_ref,
                cos_ref, sin_ref, dmask_ref, qdec_ref, kdec_ref, cdec_ref, wout_ref, gffn_ref, wrt_ref, brt_ref, tri_ref,
                o_ref, info_ref, cnt_ref, xtail, qtail, hcar, state, ybuf, rcount,
                *, tiles_per_seq):
    ts = x_ref.shape[1]
    d_lru = lam_ref.shape[1]
    d_ret = qdec_ref.shape[1]
    hd = d_ret // RET_HEADS
    chunk = dmask_ref.shape[1]
    half = d_lru // 2
    i = pl.program_id(0)

    @pl.when(i == 0)
    def _():
        rcount[...] = jnp.zeros_like(rcount)

    @pl.when(lax.rem(i, tiles_per_seq) == 0)
    def _():
        xtail[...] = jnp.zeros_like(xtail)
        qtail[...] = jnp.zeros_like(qtail)
        hcar[...] = jnp.zeros_like(hcar)
        state[...] = jnp.zeros_like(state)

    x = x_ref[0]
    h = _rms(x, gmix_ref[...]).astype(BF16)

    def proj(lo, width):
        return jnp.dot(h, win_ref[:, lo:lo + width], preferred_element_type=F32)

    xl = proj(0, d_lru)

    assert CONV_W == 4
    x3 = _group_view(xl)
    xm1 = _shift_rows(x3, 1, xtail[...])
    q3 = xm1 * convw_ref[0:1, :] + x3 * convw_ref[1:2, :]
    p3 = xm1 * convw_ref[2:3, :] + x3 * convw_ref[3:4, :]
    xc = ((convb_ref[...] + _shift_rows(q3, 2, qtail[...])) + p3).reshape(ts, d_lru)
    xtail[...] = x3[-1]
    qtail[...] = q3[-1]

    xcb = xc.astype(BF16)
    g0 = jnp.dot(xcb[:, :half], wg_ref[0], preferred_element_type=F32)
    g1 = jnp.dot(xcb[:, half:], wg_ref[1], preferred_element_type=F32)
    ra = jnp.concatenate([g0[:, :half], g1[:, :half]], axis=1) + bg_ref[:, :d_lru]
    ix = jnp.concatenate([g0[:, half:], g1[:, half:]], axis=1) + bg_ref[:, d_lru:]
    r = _sigmoid(ra)
    i_gate = _sigmoid(ix)
    z = -lam_ref[...]
    softplus = jnp.maximum(z, 0.0) + jnp.log1p(jnp.exp(-jnp.abs(z)))
    log_a = (-LRU_C * r) * softplus
    a = jnp.exp(log_a)
    om = 1.0 - a * a
    u = jnp.where(om > 0.0, om * lax.rsqrt(om), 0.0) * (i_gate * xc)
    hseq = _linear_scan(a, u, hcar[0:1, :])
    hcar[0:1, :] = hseq[ts - 1:ts, :]
    gl = proj(d_lru, d_lru)
    ybuf[:, 0:d_lru] = (hseq * jax.nn.gelu(gl)).astype(BF16)

    base = 2 * d_lru
    q = proj(base, d_ret)
    k = proj(base + d_ret, d_ret)
    v = proj(base + 2 * d_ret, d_ret)
    gr = proj(base + 3 * d_ret, d_ret)
    scale = hd ** -0.5
    for hh in range(RET_HEADS):
        ls = slice(hh * hd, (hh + 1) * hd)
        for c in range(ts // chunk):
            rs = slice(c * chunk, (c + 1) * chunk)
            cosv = cos_ref[rs, :]
            sinv = sin_ref[rs, :]
            qh = q[rs, ls]
            kh = k[rs, ls]
            qr = qh * cosv + pltpu.roll(qh, hd // 2, 1) * sinv
            kr = (kh * cosv + pltpu.roll(kh, hd // 2, 1) * sinv) * scale
            vb = v[rs, ls].astype(BF16)
            scores = lax.dot_general(qr.astype(BF16), kr.astype(BF16), (((1,), (1,)), ((), ())),
                                     preferred_element_type=F32)
            inner = jnp.dot((scores * dmask_ref[hh]).astype(BF16), vb, preferred_element_type=F32)
            st = state[hh]
            cross = jnp.dot((qr * qdec_ref[:, ls]).astype(BF16), st.astype(BF16), preferred_element_type=F32)
            kv = lax.dot_general((kr * kdec_ref[:, ls]).astype(BF16), vb, (((0,), (0,)), ((), ())),
                                 preferred_element_type=F32)
            state[hh] = st * cdec_ref[:, ls] + kv
            o = inner + cross
            mu = jnp.mean(o, axis=-1, keepdims=True)
            oc = o - mu
            var = jnp.mean(oc * oc, axis=-1, keepdims=True)
            on = oc * lax.rsqrt(var + EPS)
            grh = gr[rs, ls]
            ybuf[rs, d_lru + hh * hd:d_lru + (hh + 1) * hd] = ((grh * _sigmoid(grh)) * on).astype(BF16)

    x1 = x + jnp.dot(ybuf[...], wout_ref[...], preferred_element_type=F32)
    o_ref[0] = x1

    info_ref[...] = _route_tile(_rms(x1, gffn_ref[...]), wrt_ref, brt_ref, tri_ref, rcount)
    cnt_ref[...] = rcount[...]


def _mix(x, g_mix, w_in, conv_w, conv_b, lru_wa, lru_ba, lru_wx, lru_bx, lam, w_out, g_ffn, wrt, brt, tri):
    B, S, D = x.shape
    assert B * S < (1 << 24)
    d_lru = lam.shape[0]
    d_ret = (w_in.shape[1] - 2 * d_lru) // 4
    hd = d_ret // RET_HEADS
    ts, chunk = MIX_TILE, RET_CHUNK
    assert S % ts == 0 and ts % chunk == 0 and d_lru % (2 * LANES) == 0 and hd == LANES
    nb, bw = lru_wa.shape[0], lru_wa.shape[1]
    half = d_lru // 2
    per_half = nb // 2

    def blockdiag(w):
        out = jnp.zeros((half, half), F32)
        for j in range(per_half):
            out = out.at[j * bw:(j + 1) * bw, j * bw:(j + 1) * bw].set(w[j])
        return out
    wg = jnp.stack([jnp.concatenate([blockdiag(lru_wa[hf * per_half:(hf + 1) * per_half]),
                                     blockdiag(lru_wx[hf * per_half:(hf + 1) * per_half])], axis=1)
                    for hf in range(2)]).astype(BF16)
    bg = jnp.concatenate([lru_ba.reshape(1, d_lru), lru_bx.reshape(1, d_lru)], axis=1)

    inv = ROPE_BASE ** (-jnp.arange(hd // 2, dtype=F32) / (hd // 2))
    ang = jnp.arange(S, dtype=F32)[:, None] * inv[None, :]
    cos2 = jnp.concatenate([jnp.cos(ang), jnp.cos(ang)], axis=1)
    sin2 = jnp.concatenate([-jnp.sin(ang), jnp.sin(ang)], axis=1)
    log_g = jnp.log(1.0 - 2.0 ** (-5.0 - jnp.arange(RET_HEADS, dtype=F32)))
    idx = jnp.arange(chunk, dtype=F32)
    diff = idx[:, None] - idx[None, :]
    dmask = jnp.where(diff >= 0, jnp.exp(jnp.maximum(diff, 0.0)[None] * log_g[:, None, None]), 0.0)
    rep = lambda t: jnp.repeat(t, hd, axis=-1)
    qdec = rep(jnp.exp((idx + 1.0)[:, None] * log_g[None, :]))
    kdec = rep(jnp.exp((chunk - 1.0 - idx)[:, None] * log_g[None, :]))
    cdec = rep(jnp.exp(chunk * log_g)[None, :])

    tps = S // ts
    n_tiles = B * tps
    full = lambda shape: pl.BlockSpec(shape, lambda i: (0,) * len(shape))
    return pl.pallas_call(
        functools.partial(_mix_kernel, tiles_per_seq=tps),
        name="mix",
        grid=(n_tiles,),
        in_specs=[
            pl.BlockSpec((1, ts, D), lambda i: (i // tps, i % tps, 0)),
            full((1, D)),
            full(w_in.shape),
            full((CONV_W, d_lru)),
            full((1, d_lru)),
            full(wg.shape),
            full(bg.shape),
            full((1, d_lru)),
            pl.BlockSpec((ts, hd), lambda i: (i % tps, 0)),
            pl.BlockSpec((ts, hd), lambda i: (i % tps, 0)),
            full(dmask.shape),
            full(qdec.shape),
            full(kdec.shape),
            full(cdec.shape),
            full(w_out.shape),
            full((1, D)),
            full(wrt.shape),
            full(brt.shape),
            full(tri.shape),
        ],
        out_specs=[
            pl.BlockSpec((1, ts, D), lambda i: (i // tps, i % tps, 0)),
            pl.BlockSpec((SUBLANES, ts), lambda i: (0, i)),
            full((LANES, LANES)),
        ],
        out_shape=[
            jax.ShapeDtypeStruct((B, S, D), F32),
            jax.ShapeDtypeStruct((SUBLANES, B * S), F32),
            jax.ShapeDtypeStruct((LANES, LANES), F32),
        ],
        scratch_shapes=[
            pltpu.VMEM((SUBLANES, d_lru), F32),
            pltpu.VMEM((SUBLANES, d_lru), F32),
            pltpu.VMEM((SUBLANES, d_lru), F32),
            pltpu.VMEM((RET_HEADS, hd, hd), F32),
            pltpu.VMEM((ts, d_lru + d_ret), BF16),
            pltpu.VMEM((LANES, LANES), F32),
        ],
        compiler_params=pltpu.CompilerParams(
            dimension_semantics=("arbitrary",), vmem_limit_bytes=VMEM_LIMIT),
    )(x, g_mix.reshape(1, D), w_in.astype(BF16), conv_w, conv_b.reshape(1, d_lru), wg, bg,
      lam.reshape(1, d_lru), cos2, sin2, dmask, qdec, kdec, cdec, w_out.astype(BF16), g_ffn.reshape(1, D), wrt, brt, tri)


def _route_tile(hf, wrt_ref, brt_ref, tri_ref, carry):
    tm = hf.shape[0]
    nt = lambda a, b: lax.dot_general(a, b, (((1,), (1,)), ((), ())), preferred_element_type=F32)
    h_hi = hf.astype(BF16)
    h_lo = (hf - h_hi.astype(F32)).astype(BF16)
    both = nt(wrt_ref[...], h_hi)
    logits = (both[0:LANES] + (both[LANES:] + nt(wrt_ref[0:LANES, :], h_lo))) + brt_ref[...]
    row = lax.broadcasted_iota(I32, logits.shape, 0)
    neg = jnp.float32(-jnp.inf)
    g_row = row - N_EXPERTS
    gmask = (g_row >= 0) & (g_row < N_GROUPS)
    gl = jnp.where(gmask, logits, neg)
    gmax = jnp.max(gl, axis=0, keepdims=True)
    g_sel = jnp.min(jnp.where(gl == gmax, g_row, LANES), axis=0, keepdims=True)
    g_w = 1.0 / jnp.sum(jnp.where(gmask, jnp.exp(logits - gmax), 0.0), axis=0, keepdims=True)

    emask = (row < N_EXPERTS) & ((row >> 3) == g_sel)
    el = jnp.where(emask, logits, neg)
    v1 = jnp.max(el, axis=0, keepdims=True)
    i1 = jnp.min(jnp.where(el == v1, row, LANES), axis=0, keepdims=True)
    el2 = jnp.where(row == i1, neg, el)
    v2 = jnp.max(el2, axis=0, keepdims=True)
    i2 = jnp.min(jnp.where(el2 == v2, row, LANES), axis=0, keepdims=True)
    t = jnp.exp(v2 - v1)
    w1 = g_w / (1.0 + t)
    w2 = g_w * (t / (1.0 + t))

    onehot = jnp.where((row == i1) | (row == i2), 1.0, 0.0).astype(BF16)
    before = jnp.concatenate([carry[...]] * (tm // LANES), axis=1)
    prefix = jnp.dot(onehot, tri_ref[...], preferred_element_type=F32) + before
    r1 = jnp.sum(jnp.where(row == i1, prefix, 0.0), axis=0, keepdims=True)
    r2 = jnp.sum(jnp.where(row == i2, prefix, 0.0), axis=0, keepdims=True)
    carry[...] = carry[...] + jnp.dot(onehot, jnp.ones((tm, LANES), BF16), preferred_element_type=F32)

    zero = jnp.zeros_like(w1)
    return jnp.concatenate([i1.astype(F32), i2.astype(F32), w1, w2, r1, r2, zero, zero], axis=0)


def _router_weights(w_grp, b_grp, w_exp, b_exp, tm):
    D = w_grp.shape[0]
    assert N_GROUPS + N_EXPERTS <= LANES and EXPERTS_PER_GROUP == 8
    pad = LANES - N_GROUPS - N_EXPERTS
    wrt = jnp.concatenate([w_exp.T, w_grp.T, jnp.zeros((pad, D), F32)], axis=0)
    wrt_hi = wrt.astype(BF16)
    wrt = jnp.concatenate([wrt_hi, (wrt - wrt_hi.astype(F32)).astype(BF16)], axis=0)
    brt = jnp.broadcast_to(jnp.concatenate([b_exp, b_grp, jnp.zeros((pad,), F32)])[:, None], (LANES, tm))
    idx = jnp.arange(tm)
    tri = (idx[:, None] < idx[None, :]).astype(BF16)
    return wrt, brt, tri


def _zero_fill(start, padfirst_ref, padlen_ref, tail_ref, zbuf, xs_ref, zsem):
    bm = EXPERT_BLOCK

    def go(slot0, nslots):
        c = pltpu.make_async_copy(
            zbuf.at[pl.ds(0, nslots * TOKEN_ROWS), :],
            xs_ref.at[pl.ds(pl.multiple_of(slot0 * TOKEN_ROWS, TOKEN_ROWS), nslots * TOKEN_ROWS), :], zsem)
        if start:
            c.start()
        else:
            c.wait()

    def per_expert(e, carry):
        off = padfirst_ref[e]
        length = padlen_ref[e]
        for bit in reversed(range(bm.bit_length() - 1)):
            take = (length >> bit) & 1

            @pl.when(take == 1)
            def _():
                go(off, 1 << bit)
            off = off + (take << bit)
        return carry

    lax.fori_loop(0, N_EXPERTS, per_expert, 0)

    def per_tail(k, carry):
        go((tail_ref[0] + k) * bm, bm)
        return carry

    lax.fori_loop(0, tail_ref[1], per_tail, 0)


def _dispatch_kernel(padfirst_ref, padlen_ref, tail_ref, dest_ref, x1_ref, g_ref, xs_ref, tbuf, zbuf, sems, zsem):
    tm = x1_ref.shape[0]
    i = pl.program_id(0)
    s = lax.rem(i, 2)

    @pl.when(i == 0)
    def _():
        zbuf[...] = jnp.zeros_like(zbuf)
        _zero_fill(True, padfirst_ref, padlen_ref, tail_ref, zbuf, xs_ref, zsem)

    def wait_all(slot):
        for _ in range(2 * tm):
            _tile_copy(tbuf.at[slot], 0, xs_ref, 0, sems.at[slot]).wait()

    @pl.when(i >= 2)
    def _():
        wait_all(s)

    _store_token_tiles(tbuf.at[s], _rms(x1_ref[...], g_ref[...]))
    for t in range(tm):
        _tile_copy(tbuf.at[s], t * TOKEN_ROWS, xs_ref, dest_ref[t], sems.at[s]).start(priority=0)
        _tile_copy(tbuf.at[s], t * TOKEN_ROWS, xs_ref, dest_ref[tm + t], sems.at[s]).start(priority=1)

    @pl.when(i == pl.num_programs(0) - 1)
    def _():
        @pl.when(i >= 1)
        def _():
            wait_all(1 - s)
        wait_all(s)
        _zero_fill(False, padfirst_ref, padlen_ref, tail_ref, zbuf, xs_ref, zsem)


def _dispatch(pad_first, pad_len, tail, dest, x1, g_ffn, n_slots):
    T, D = x1.shape
    tm = DISPATCH_TILE
    assert T % tm == 0 and D == 2 * TOKEN_ROWS * LANES
    grid_spec = pltpu.PrefetchScalarGridSpec(
        num_scalar_prefetch=3,
        grid=(T // tm,),
        in_specs=[
            pl.BlockSpec((2 * tm,), lambda i, *_: (i,), memory_space=pltpu.SMEM),
            pl.BlockSpec((tm, D), lambda i, *_: (i, 0)),
            pl.BlockSpec((1, D), lambda i, *_: (0, 0)),
        ],
        out_specs=pl.BlockSpec(memory_space=pl.ANY),
        scratch_shapes=[
            pltpu.VMEM((2, tm * TOKEN_ROWS, LANES), U32),
            pltpu.VMEM((EXPERT_BLOCK * TOKEN_ROWS, LANES), U32),
            pltpu.SemaphoreType.DMA((2,)),
            pltpu.SemaphoreType.DMA,
        ],
    )
    return pl.pallas_call(
        _dispatch_kernel,
        name="dispatch",
        grid_spec=grid_spec,
        out_shape=jax.ShapeDtypeStruct((n_slots * TOKEN_ROWS, LANES), U32),
        compiler_params=pltpu.CompilerParams(
            dimension_semantics=("arbitrary",), has_side_effects=True, vmem_limit_bytes=VMEM_LIMIT),
    )(pad_first, pad_len, tail, dest, x1, g_ffn.reshape(1, D))


def _experts_kernel(first_ref, count_ref, xs_in_ref, w1_hbm, w3_hbm, w2_hbm, ys_ref,
                    w1f, w3f, w2f, w1b, w3b, w2b, xbuf, ybuf, xb, sem_w, sem_in, sem_out):
    del xs_in_ref
    e = pl.program_id(0)
    n_exp = pl.num_programs(0)
    bm = xb.shape[0]
    rows = bm * TOKEN_ROWS
    first = first_ref[e]
    count = count_ref[e]
    total = first_ref[n_exp - 1] + count_ref[n_exp - 1]
    wslot = lax.rem(e, 2)

    def weight_copies(ex, s):
        return [pltpu.make_async_copy(hbm.at[ex], buf.at[s], sem_w.at[s])
                for hbm, buf in ((w1_hbm, w1f), (w3_hbm, w3f), (w2_hbm, w2f))]

    def block_rows(g):
        return pl.ds(pl.multiple_of(g * rows, rows), rows)

    def in_copy(g):
        s = lax.rem(g, IN_RING)
        return pltpu.make_async_copy(ys_ref.at[block_rows(g), :], xbuf.at[s], sem_in.at[s])

    def out_copy(g):
        s = lax.rem(g, OUT_RING)
        return pltpu.make_async_copy(ybuf.at[s], ys_ref.at[block_rows(g), :], sem_out.at[s])

    @pl.when(e == 0)
    def _():
        for c in weight_copies(0, 0):
            c.start(priority=1)
        for k in range(IN_RING - 1):
            @pl.when(k < total)
            def _():
                in_copy(k).start()

    @pl.when(e + 1 < n_exp)
    def _():
        for c in weight_copies(e + 1, 1 - wslot):
            c.start(priority=1)

    for c in weight_copies(e, wslot):
        c.wait()
    w1b[...] = w1f[wslot].astype(BF16)
    w3b[...] = w3f[wslot].astype(BF16)
    w2b[...] = w2f[wslot].astype(BF16)

    def body(j, carry):
        g = first + j
        in_copy(g).wait()

        @pl.when(g + IN_RING - 1 < total)
        def _():
            in_copy(g + IN_RING - 1).start()

        @pl.when(g >= OUT_RING)
        def _():
            out_copy(g - OUT_RING).wait()

        s = lax.rem(g, IN_RING)
        half = TOKEN_ROWS * LANES
        for r in range(TOKEN_ROWS):
            hi, lo = _load_token_pair(xbuf.at[s], r, bm)
            xb[:, r * LANES:(r + 1) * LANES] = hi.astype(BF16)
            xb[:, half + r * LANES:half + (r + 1) * LANES] = lo.astype(BF16)
        x = xb[...]
        a1 = jnp.dot(x, w1b[...], preferred_element_type=F32)
        a3 = jnp.dot(x, w3b[...], preferred_element_type=F32)
        hmid = ((a1 * _sigmoid(a1)) * a3).astype(BF16)
        _store_token_tiles(ybuf.at[lax.rem(g, OUT_RING)], jnp.dot(hmid, w2b[...], preferred_element_type=F32))
        out_copy(g).start()
        return carry

    lax.fori_loop(0, count, body, 0)

    @pl.when(e == n_exp - 1)
    def _():
        for k in range(OUT_RING):
            @pl.when(total - 1 - k >= 0)
            def _():
                out_copy(total - 1 - k).wait()


def _experts(first_block, block_count, xs, w1, w3, w2):
    bm = EXPERT_BLOCK
    E, D, DE = w1.shape
    assert D == 2 * TOKEN_ROWS * LANES
    rows = bm * TOKEN_ROWS
    grid_spec = pltpu.PrefetchScalarGridSpec(
        num_scalar_prefetch=2,
        grid=(E,),
        in_specs=[
            pl.BlockSpec(memory_space=pl.ANY),
            pl.BlockSpec(memory_space=pl.ANY),
            pl.BlockSpec(memory_space=pl.ANY),
            pl.BlockSpec(memory_space=pl.ANY),
        ],
        out_specs=pl.BlockSpec(memory_space=pl.ANY),
        scratch_shapes=[
            pltpu.VMEM((2, D, DE), F32),
            pltpu.VMEM((2, D, DE), F32),
            pltpu.VMEM((2, DE, D), F32),
            pltpu.VMEM((D, DE), BF16),
            pltpu.VMEM((D, DE), BF16),
            pltpu.VMEM((DE, D), BF16),
            pltpu.VMEM((IN_RING, rows, LANES), U32),
            pltpu.VMEM((OUT_RING, rows, LANES), U32),
            pltpu.VMEM((bm, D), BF16),
            pltpu.SemaphoreType.DMA((2,)),
            pltpu.SemaphoreType.DMA((IN_RING,)),
            pltpu.SemaphoreType.DMA((OUT_RING,)),
        ],
    )
    return pl.pallas_call(
        _experts_kernel,
        name="experts",
        grid_spec=grid_spec,
        out_shape=jax.ShapeDtypeStruct(xs.shape, U32),
        input_output_aliases={2: 0},
        compiler_params=pltpu.CompilerParams(
            dimension_semantics=("arbitrary",), has_side_effects=True, vmem_limit_bytes=VMEM_LIMIT),
    )(first_block, block_count, xs, w1, w3, w2)


def _combine_kernel(dest0_ref, dest1_ref, dest2_ref, x1_ref, p_ref, info_ref, ys_ref, gple_ref, wgate_ref, bgate_ref,
                    wproj_ref, gfin_ref, o_ref, *scratch, final_norm):
    gbufs, sems = scratch[:GATHER_SLOTS], scratch[GATHER_SLOTS]
    tm = x1_ref.shape[0]
    i = pl.program_id(0)
    last = pl.num_programs(0) - 1

    def start_pair(idx_ref, t, s):
        _tile_copy(ys_ref, idx_ref[t], gbufs[s].at[0], t * TOKEN_ROWS, sems.at[s]).start(priority=0)
        _tile_copy(ys_ref, idx_ref[tm + t], gbufs[s].at[1], t * TOKEN_ROWS, sems.at[s]).start(priority=1)

    def wait_all(s):
        for _ in range(2 * tm):
            _tile_copy(ys_ref, 0, gbufs[s].at[0], 0, sems.at[s]).wait()

    @pl.when(i == 0)
    def _():
        def body(t, c):
            start_pair(dest0_ref, t, 0)
            start_pair(dest1_ref, t, 1)
            return c
        lax.fori_loop(0, tm, body, 0, unroll=DMA_UNROLL)

    def step(slot):
        ahead = (slot + GATHER_SLOTS - 1) % GATHER_SLOTS
        wait_all(slot)
        for t in range(tm):
            start_pair(dest2_ref, t, ahead)

        info = info_ref[...].T
        w_a = info[:, 2:3]
        w_b = info[:, 3:4]
        his, los = [], []
        for j in range(TOKEN_ROWS):
            a_hi, a_lo = _load_token_pair(gbufs[slot].at[0], j, tm)
            b_hi, b_lo = _load_token_pair(gbufs[slot].at[1], j, tm)
            his.append(a_hi * w_a + b_hi * w_b)
            los.append(a_lo * w_a + b_lo * w_b)
        x2 = x1_ref[...] + jnp.concatenate(his + los, axis=1)
        pp = jnp.dot(p_ref[...].astype(BF16), wproj_ref[...], preferred_element_type=F32)
        gate = _sigmoid(jnp.dot(_rms(x2, gple_ref[...]).astype(BF16), wgate_ref[...], preferred_element_type=F32)
                        + bgate_ref[...])
        x3 = x2 + gate * pp
        o_ref[...] = _rms(x3, gfin_ref[...]) if final_norm else x3

        @pl.when(i == last)
        def _():
            wait_all((slot + 1) % GATHER_SLOTS)
            wait_all(ahead)

    for slot in range(GATHER_SLOTS):
        pl.when(lax.rem(i, GATHER_SLOTS) == slot)(functools.partial(step, slot))


def _combine(dest, x1, p, layer, info, ys, g_ple, w_gate, b_gate, w_proj, g_final, final_norm):
    T, D = x1.shape
    S, PD = p.shape[2], p.shape[3]
    tm = COMBINE_TILE
    assert T % tm == 0 and S % tm == 0 and D == 2 * TOKEN_ROWS * LANES
    n_steps = T // tm
    tps = S // tm
    const = lambda shape: pl.BlockSpec(shape, lambda i: (0,) * len(shape))
    return pl.pallas_call(
        functools.partial(_combine_kernel, final_norm=final_norm),
        name="combine",
        grid=(n_steps,),
        in_specs=[
            pl.BlockSpec((2 * tm,), lambda i: (i,), memory_space=pltpu.SMEM),
            pl.BlockSpec((2 * tm,), lambda i: (jnp.minimum(i + 1, n_steps - 1),), memory_space=pltpu.SMEM),
            pl.BlockSpec((2 * tm,), lambda i: (jnp.minimum(i + 2, n_steps - 1),), memory_space=pltpu.SMEM),
            pl.BlockSpec((tm, D), lambda i: (i, 0)),
            pl.BlockSpec((None, None, tm, PD), lambda i: (layer, i // tps, i % tps, 0)),
            pl.BlockSpec((SUBLANES, tm), lambda i: (0, i)),
            pl.BlockSpec(memory_space=pl.ANY),
            const((1, D)),
            const((D, D)),
            const((1, D)),
            const((PD, D)),
            const((1, D)),
        ],
        out_specs=pl.BlockSpec((tm, D), lambda i: (i, 0)),
        out_shape=jax.ShapeDtypeStruct((T, D), F32),
        scratch_shapes=[pltpu.VMEM((2, tm * TOKEN_ROWS, LANES), U32) for _ in range(GATHER_SLOTS)]
        + [pltpu.SemaphoreType.DMA((GATHER_SLOTS,))],
        compiler_params=pltpu.CompilerParams(
            dimension_semantics=("arbitrary",), vmem_limit_bytes=VMEM_LIMIT),
    )(dest, dest, dest, x1, p, info, ys, g_ple.reshape(1, D), w_gate.astype(BF16), b_gate.reshape(1, D),
      w_proj.astype(BF16), g_final.reshape(1, D))


def _slot_plan(info_t, counts_f, n_assign):
    bm = EXPERT_BLOCK
    counts = counts_f[:N_EXPERTS, 0].astype(I32)
    padded = (counts + bm - 1) // bm * bm
    pad_end = jnp.cumsum(padded)
    pad_start = pad_end - padded
    n_blocks = -(-n_assign // bm) + N_EXPERTS
    e = info_t[0:2, :].astype(I32)
    rank = info_t[4:6, :].astype(I32)
    hit = e[:, :, None] == jnp.arange(N_EXPERTS, dtype=I32)[None, None, :]
    dest = jnp.sum(jnp.where(hit, pad_start[None, None, :], 0), axis=-1) + rank
    T = dest.shape[1]
    dest = dest.reshape(2, T // SLOT_TILE, SLOT_TILE).transpose(1, 0, 2).reshape(-1)
    tail = jnp.stack([pad_end[-1] // bm, n_blocks - pad_end[-1] // bm]).astype(I32)
    plan = dict(
        dest=dest * TOKEN_ROWS,
        first_block=pad_start // bm,
        block_count=padded // bm,
        pad_first=pad_start + counts,
        pad_len=padded - counts,
        tail=tail,
    )
    return plan, n_blocks * bm


def kernel(x, p, g_mix, w_in, conv_w, conv_b, lru_wa, lru_ba, lru_wx, lru_bx, lru_lambda, w_out, g_ffn, w_router_group, b_router_group, w_router_expert, b_router_expert, w1, w3, w2, g_ple, w_ple_gate, b_ple_gate, w_ple_proj, g_final):
    B, S, D = x.shape
    T = B * S
    depth = g_mix.shape[0]
    for l in range(depth):
        wrt, brt, tri = _router_weights(w_router_group[l], b_router_group[l], w_router_expert[l],
                                        b_router_expert[l], MIX_TILE)
        x1, info_t, counts = _mix(x, g_mix[l], w_in[l], conv_w[l], conv_b[l], lru_wa[l], lru_ba[l], lru_wx[l],
                                  lru_bx[l], lru_lambda[l], w_out[l], g_ffn[l], wrt, brt, tri)
        x1 = x1.reshape(T, D)
        plan, n_slots = _slot_plan(info_t, counts, 2 * T)
        xs = _dispatch(plan["pad_first"], plan["pad_len"], plan["tail"], plan["dest"], x1, g_ffn[l], n_slots)
        ys = _experts(plan["first_block"], plan["block_count"], xs, w1[l], w3[l], w2[l])
        x = _combine(plan["dest"], x1, p, l, info_t, ys, g_ple[l], w_ple_gate[l], b_ple_gate[l],
                     w_ple_proj[l], g_final, l == depth - 1).reshape(B, S, D)
    return x
```
